```python
import math
import jax, jax.numpy as jnp
from jax import lax
import numpy as np

D_MODEL = 1024
BATCH = 8
SEQ = 2048
DEPTH = 4
DEC_BATCH = 128
DEC_SEQ = 1
PAST_LEN = 2048
PAGE_SIZE = 128

N_MIXERS = 3
N_RET_LAYERS = (DEPTH + 2) // 3
N_POOL_LAYERS = (DEPTH + 1) // 3
N_DSA_LAYERS = DEPTH // 3
RET_HEADS = 4
RET_KEY_DIM = D_MODEL // RET_HEADS
RET_VAL_DIM = 2 * RET_KEY_DIM
RET_CHUNK = 128
RET_IN = 2 * RET_HEADS * RET_KEY_DIM + 2 * RET_HEADS * RET_VAL_DIM
POOL_WINDOWS = (2, 4, 8, 16)
POOL_GROUP_DIM = D_MODEL // len(POOL_WINDOWS)
POOL_HIST = max(POOL_WINDOWS) - 1
ATTN_HEADS = 8
ATTN_HEAD_DIM = D_MODEL // ATTN_HEADS
IDX_HEADS = 4
IDX_DIM = 64
IDX_SCALE = (IDX_HEADS ** -0.5) * (IDX_DIM ** -0.5)
TOPK_MAX = 256
DSA_Q_BLOCK = 16
DSA_IN = 3 * D_MODEL + IDX_HEADS * IDX_DIM + IDX_DIM + IDX_HEADS
FFN_DIM = -(-8 * D_MODEL // (3 * 256)) * 256
ROPE_THETA = 10000.0
LN_EPS = 1e-5
DEEPNORM_ALPHA = (2.0 * DEPTH) ** 0.25
DEEPNORM_BETA = (8.0 * DEPTH) ** -0.25

kernel_name = "hybrid_retention_pool_dsa_decoder_step"

F32 = jnp.float32


def rope(x, pos):
    half = x.shape[-1] // 2
    inv = jnp.power(ROPE_THETA, -jnp.arange(half, dtype=F32) / half)
    ang = pos.astype(F32)[:, None] * inv[None, :]
    cos = jnp.cos(ang)[None, :, None, :]
    sin = jnp.sin(ang)[None, :, None, :]
    xf = x.astype(F32)
    x1, x2 = xf[..., :half], xf[..., half:]
    return jnp.concatenate([x1 * cos - x2 * sin, x2 * cos + x1 * sin], axis=-1).astype(x.dtype)


def layer_norm(x, g, b):
    xf = x.astype(F32)
    mu = jnp.mean(xf, axis=-1, keepdims=True)
    var = jnp.mean(jnp.square(xf - mu), axis=-1, keepdims=True)
    return ((xf - mu) * lax.rsqrt(var + LN_EPS) * g.astype(F32) + b.astype(F32)).astype(x.dtype)


def deepnorm_residual(x, h, g, b):
    return layer_norm(DEEPNORM_ALPHA * x + h, g, b)


def swiglu(x, w_in, w_out):
    a, u = jnp.split(x @ w_in, 2, axis=-1)
    return (jax.nn.silu(a) * u) @ w_out


def retention_chunked(q, k, v, s0):
    b, l, h, _ = q.shape
    c = math.gcd(l, RET_CHUNK)
    n = l // c
    lg = jnp.log1p(-jnp.exp2(-5.0 - jnp.arange(h, dtype=F32)))
    i = jnp.arange(c, dtype=F32)
    diff = i[:, None] - i[None, :]
    decay_in = jnp.where(diff[None] >= 0, jnp.exp(jnp.maximum(diff, 0.0)[None] * lg[:, None, None]), 0.0)
    q_dec = jnp.exp((i[None, :] + 1.0) * lg[:, None]).T[None, :, :, None]
    k_dec = jnp.exp((c - 1.0 - i[None, :]) * lg[:, None]).T[None, :, :, None]
    c_dec = jnp.exp(c * lg)[None, :, None, None]

    def to_chunks(a):
        return a.reshape(b, n, c, *a.shape[2:]).swapaxes(0, 1)

    def step(s, blk):
        qc, kc, vc = blk
        att = jnp.einsum('bihd,bjhd->bhij', qc, kc) * decay_in
        inner = jnp.einsum('bhij,bjhe->bihe', att, vc)
        cross = jnp.einsum('bihd,bhde->bihe', qc * q_dec, s)
        s_new = s * c_dec + jnp.einsum('bjhd,bjhe->bhde', kc * k_dec, vc)
        return s_new, inner + cross

    s_fin, o = lax.scan(step, s0, (to_chunks(q), to_chunks(k), to_chunks(v)))
    return o.swapaxes(0, 1).reshape(b, l, h, -1), s_fin


def retention_mixer(x, s0, pos, w_in, norm_g, w_out):
    b, l, _ = x.shape
    hk = RET_HEADS * RET_KEY_DIM
    hv = RET_HEADS * RET_VAL_DIM
    q, k, v, g = jnp.split(x @ w_in, [hk, 2 * hk, 2 * hk + hv], axis=-1)
    q = rope(q.reshape(b, l, RET_HEADS, RET_KEY_DIM), pos).astype(F32)
    k = rope(k.reshape(b, l, RET_HEADS, RET_KEY_DIM), pos).astype(F32) * (RET_KEY_DIM ** -0.5)
    v = v.reshape(b, l, RET_HEADS, RET_VAL_DIM).astype(F32)
    o, s_fin = retention_chunked(q, k, v, s0.astype(F32))
    mu = jnp.mean(o, axis=-1, keepdims=True)
    var = jnp.mean(jnp.square(o - mu), axis=-1, keepdims=True)
    o = ((o - mu) * lax.rsqrt(var + LN_EPS)).reshape(b, l, hv) * norm_g.astype(F32)
    y = (jax.nn.silu(g.astype(F32)) * o).astype(x.dtype) @ w_out
    return y, s_fin


def pool_mixer(x_hist, x_new, pos0, w_grp, scale):
    l = x_new.shape[1]
    x_ext = jnp.concatenate([x_hist.astype(x_new.dtype), x_new], axis=1)
    xf = x_ext.astype(F32)
    cs = jnp.pad(jnp.cumsum(xf, axis=1), ((0, 0), (1, 0), (0, 0)))
    pos = pos0 + jnp.arange(l)
    outs = []
    for gi, w in enumerate(POOL_WINDOWS):
        lo, hi = gi * POOL_GROUP_DIM, (gi + 1) * POOL_GROUP_DIM
        wsum = cs[:, POOL_HIST + 1:POOL_HIST + 1 + l, lo:hi] - cs[:, POOL_HIST + 1 - w:POOL_HIST + 1 - w + l, lo:hi]
        cnt = jnp.minimum(w, pos + 1).astype(F32)
        d = wsum / cnt[None, :, None] - xf[:, POOL_HIST:, lo:hi]
        outs.append(jnp.einsum('bld,de->ble', d, w_grp[gi].astype(F32)))
    y = (jnp.concatenate(outs, axis=-1) * scale.astype(F32)).astype(x_new.dtype)
    return y, x_ext[:, -POOL_HIST:]


def dsa_project(x, pos, w_in):
    b, l, _ = x.shape
    d = D_MODEL
    q, k, v, qi, ki, wi = jnp.split(
        x @ w_in, [d, 2 * d, 3 * d, 3 * d + IDX_HEADS * IDX_DIM, 3 * d + IDX_HEADS * IDX_DIM + IDX_DIM], axis=-1)
    q = rope(q.reshape(b, l, ATTN_HEADS, ATTN_HEAD_DIM), pos)
    k = rope(k.reshape(b, l, ATTN_HEADS, ATTN_HEAD_DIM), pos)
    v = v.reshape(b, l, ATTN_HEADS, ATTN_HEAD_DIM)
    qi = rope(qi.reshape(b, l, IDX_HEADS, IDX_DIM), pos)
    ki = rope(ki.reshape(b, l, 1, IDX_DIM), pos)[:, :, 0, :]
    return q, k, v, qi, ki, wi


def indexer_topk(qi, wi, ki_all, pos_q, n_top):
    rel = jax.nn.relu(jnp.einsum('bqhd,bkd->bqhk', qi.astype(F32), ki_all.astype(F32)))
    score = jnp.einsum('bqh,bqhk->bqk', wi.astype(F32) * IDX_SCALE, rel)
    visible = jnp.arange(ki_all.shape[1])[None, :] <= pos_q[:, None]
    score = jnp.where(visible[None], score, -jnp.inf)
    _, idx = lax.top_k(score, n_top)
    valid = idx <= pos_q[None, :, None]
    return idx, valid


def sparse_attend(q, k_sel, v_sel, valid):
    s = jnp.einsum('bqhd,bqkhd->bqhk', q.astype(F32), k_sel.astype(F32)) * (ATTN_HEAD_DIM ** -0.5)
    s = jnp.where(valid[:, :, None, :], s, -jnp.inf)
    p = jax.nn.softmax(s, axis=-1)
    return jnp.einsum('bqhk,bqkhd->bqhd', p, v_sel.astype(F32)).astype(q.dtype)


_gather_rows = jax.vmap(lambda rows, ix: rows[ix])


def dsa_prompt(x, w_in, w_out):
    b, l, _ = x.shape
    pos = jnp.arange(l)
    q, k, v, qi, ki, wi = dsa_project(x, pos, w_in)
    n_top = min(TOPK_MAX, l // 4)
    nb = l // DSA_Q_BLOCK

    def blockify(a):
        return a.reshape(b, nb, DSA_Q_BLOCK, *a.shape[2:]).swapaxes(0, 1)

    def block(args):
        qb, qib, wib, posb = args
        idx, valid = indexer_topk(qib, wib, ki, posb, n_top)
        return sparse_attend(qb, _gather_rows(k, idx), _gather_rows(v, idx), valid)

    o = lax.map(block, (blockify(q), blockify(qi), blockify(wi), pos.reshape(nb, DSA_Q_BLOCK)))
    o = o.swapaxes(0, 1).reshape(b, l, ATTN_HEADS * ATTN_HEAD_DIM)
    return o @ w_out, k, v, ki


def dsa_sample(x, cache_k, cache_v, cache_kidx, page_table, w_in, w_out):
    b, l, _ = x.shape
    pos = PAST_LEN + jnp.arange(l)
    q, k, v, qi, ki, wi = dsa_project(x, pos, w_in)
    ki_past = cache_kidx[page_table].reshape(b, PAST_LEN, IDX_DIM)
    ki_all = jnp.concatenate([ki_past.astype(ki.dtype), ki], axis=1)
    n_top = min(TOPK_MAX, (PAST_LEN + l) // 4)
    idx, valid = indexer_topk(qi, wi, ki_all, pos, n_top)
    from_past = idx < PAST_LEN
    pidx = jnp.minimum(idx, PAST_LEN - 1)
    phys = _gather_rows(page_table, pidx // PAGE_SIZE)
    off = pidx % PAGE_SIZE
    nidx = jnp.clip(idx - PAST_LEN, 0, l - 1)
    sel = from_past[..., None, None]
    k_sel = jnp.where(sel, cache_k[phys, off], _gather_rows(k, nidx))
    v_sel = jnp.where(sel, cache_v[phys, off], _gather_rows(v, nidx))
    o = sparse_attend(q, k_sel, v_sel, valid).reshape(b, l, ATTN_HEADS * ATTN_HEAD_DIM)
    return o @ w_out, k, v, ki


def setup_inputs(seed: int = 0) -> dict:
    key = jax.random.key(seed)
    ks = jax.random.split(key, 24)
    n_pages = PAST_LEN // PAGE_SIZE
    n_used = DEC_BATCH * n_pages
    n_phys = n_used + (n_used + 3) // 4

    def nrm(k, shape, scale):
        return jax.random.normal(k, shape, F32) * scale

    hv = RET_HEADS * RET_VAL_DIM
    return {
        "x_prompt": nrm(ks[0], (BATCH, SEQ, D_MODEL), 1.0),
        "x_sample": nrm(ks[1], (DEC_BATCH, DEC_SEQ, D_MODEL), 1.0),
        "state_ret": nrm(ks[2], (N_RET_LAYERS, DEC_BATCH, RET_HEADS, RET_KEY_DIM, RET_VAL_DIM), 0.5),
        "state_pool": nrm(ks[3], (N_POOL_LAYERS, DEC_BATCH, POOL_HIST, D_MODEL), 1.0),
        "cache_k": nrm(ks[4], (N_DSA_LAYERS, n_phys, PAGE_SIZE, ATTN_HEADS, ATTN_HEAD_DIM), 1.0),
        "cache_v": nrm(ks[5], (N_DSA_LAYERS, n_phys, PAGE_SIZE, ATTN_HEADS, ATTN_HEAD_DIM), 1.0),
        "cache_kidx": nrm(ks[6], (N_DSA_LAYERS, n_phys, PAGE_SIZE, IDX_DIM), 1.0),
        "page_table": jax.random.permutation(ks[7], n_phys)[:n_used].reshape(DEC_BATCH, n_pages).astype(jnp.int32),
        "ret_w_in": nrm(ks[8], (N_RET_LAYERS, D_MODEL, RET_IN), D_MODEL ** -0.5),
        "ret_norm_g": 1.0 + nrm(ks[9], (N_RET_LAYERS, hv), 0.02),
        "ret_w_out": nrm(ks[10], (N_RET_LAYERS, hv, D_MODEL), hv ** -0.5 * DEEPNORM_BETA),
        "pool_w_grp": nrm(ks[11], (N_POOL_LAYERS, len(POOL_WINDOWS), POOL_GROUP_DIM, POOL_GROUP_DIM),
                          POOL_GROUP_DIM ** -0.5 * DEEPNORM_BETA),
        "pool_scale": 1.0 + nrm(ks[12], (N_POOL_LAYERS, D_MODEL), 0.02),
        "dsa_w_in": nrm(ks[13], (N_DSA_LAYERS, D_MODEL, DSA_IN), D_MODEL ** -0.5),
        "dsa_w_out": nrm(ks[14], (N_DSA_LAYERS, D_MODEL, D_MODEL), D_MODEL ** -0.5 * DEEPNORM_BETA),
        "ffn_w_in": nrm(ks[15], (DEPTH, D_MODEL, 2 * FFN_DIM), D_MODEL ** -0.5),
        "ffn_w_out": nrm(ks[16], (DEPTH, FFN_DIM, D_MODEL), FFN_DIM ** -0.5 * DEEPNORM_BETA),
        "ln_g": 1.0 + nrm(ks[17], (DEPTH, 2, D_MODEL), 0.02),
        "ln_b": nrm(ks[18], (DEPTH, 2, D_MODEL), 0.02),
    }


def reference(x_prompt, x_sample, state_ret, state_pool, cache_k, cache_v, cache_kidx, page_table,
              ret_w_in, ret_norm_g, ret_w_out, pool_w_grp, pool_scale, dsa_w_in, dsa_w_out,
              ffn_w_in, ffn_w_out, ln_g, ln_b):
    xp, xs = x_prompt, x_sample
    bp, lp, _ = xp.shape
    ls = xs.shape[1]
    pos_p = jnp.arange(lp)
    pos_s = PAST_LEN + jnp.arange(ls)
    ret_p, ret_s, pool_p, pool_s = [], [], [], []
    kp_l, vp_l, kip_l, ks_l, vs_l, kis_l = [], [], [], [], [], []
    for i in range(DEPTH):
        kind, j = i % N_MIXERS, i // N_MIXERS
        if kind == 0:
            s_zero = jnp.zeros((bp, RET_HEADS, RET_KEY_DIM, RET_VAL_DIM), F32)
            hp, sp = retention_mixer(xp, s_zero, pos_p, ret_w_in[j], ret_norm_g[j], ret_w_out[j])
            hs, ss = retention_mixer(xs, state_ret[j], pos_s, ret_w_in[j], ret_norm_g[j], ret_w_out[j])
            ret_p.append(sp)
            ret_s.append(ss)
        elif kind == 1:
            hist0 = jnp.zeros((bp, POOL_HIST, D_MODEL), xp.dtype)
            hp, sp = pool_mixer(hist0, xp, 0, pool_w_grp[j], pool_scale[j])
            hs, ss = pool_mixer(state_pool[j], xs, PAST_LEN, pool_w_grp[j], pool_scale[j])
            pool_p.append(sp)
            pool_s.append(ss)
        else:
            hp, kp, vp, kip = dsa_prompt(xp, dsa_w_in[j], dsa_w_out[j])
            hs, kss, vss, kis = dsa_sample(xs, cache_k[j], cache_v[j], cache_kidx[j], page_table,
                                           dsa_w_in[j], dsa_w_out[j])
            kp_l.append(kp)
            vp_l.append(vp)
            kip_l.append(kip)
            ks_l.append(kss)
            vs_l.append(vss)
            kis_l.append(kis)
        xp = deepnorm_residual(xp, hp, ln_g[i, 0], ln_b[i, 0])
        xs = deepnorm_residual(xs, hs, ln_g[i, 0], ln_b[i, 0])
        xp = deepnorm_residual(xp, swiglu(xp, ffn_w_in[i], ffn_w_out[i]), ln_g[i, 1], ln_b[i, 1])
        xs = deepnorm_residual(xs, swiglu(xs, ffn_w_in[i], ffn_w_out[i]), ln_g[i, 1], ln_b[i, 1])
    return (xp, xs, jnp.stack(ret_p), jnp.stack(ret_s), jnp.stack(pool_p), jnp.stack(pool_s),
            jnp.stack(kp_l), jnp.stack(vp_l), jnp.stack(kip_l), jnp.stack(ks_l), jnp.stack(vs_l), jnp.stack(kis_l))
```

```python
import functools
import math

import jax
import jax.numpy as jnp
from jax import lax
from jax.experimental import pallas as pl
from jax.experimental.pallas import tpu as pltpu

F32 = jnp.float32
BF16 = jnp.bfloat16
I32 = jnp.int32

LANES = 128
VMEM_LIMIT_BYTES = 56 * 1024 * 1024

D_MODEL = 1024
DEPTH = 4
PAGE_SIZE = 128
RET_HEADS = 4
RET_KEY_DIM = D_MODEL // RET_HEADS
RET_VAL_DIM = 2 * RET_KEY_DIM
RET_CHUNK = 128
POOL_WINDOWS = (2, 4, 8, 16)
POOL_GROUP_DIM = D_MODEL // len(POOL_WINDOWS)
POOL_HIST = max(POOL_WINDOWS) - 1
ATTN_HEADS = 8
ATTN_HEAD_DIM = D_MODEL // ATTN_HEADS
IDX_HEADS = 4
IDX_DIM = 64
IDX_SCALE = (IDX_HEADS ** -0.5) * (IDX_DIM ** -0.5)
TOPK_MAX = 256
FFN_DIM = -(-8 * D_MODEL // (3 * 256)) * 256
ROPE_THETA = 10000.0
LN_EPS = 1e-5
DEEPNORM_ALPHA = (2.0 * DEPTH) ** 0.25
INT32_MIN = -(2 ** 31)
NEG_INF = float("-inf")


def _params(*sem):
    return pltpu.CompilerParams(dimension_semantics=sem, vmem_limit_bytes=VMEM_LIMIT_BYTES)


def _resident(shape):
    nd = len(shape)
    return pl.BlockSpec(shape, lambda *_: (0,) * nd, pipeline_mode=pl.Buffered(1))


def _row_tile(m):
    for t in (512, 256, 128):
        if m % t == 0:
            return t
    raise ValueError(f"token count {m} is not a multiple of 128")


def _layer_norm(z, g, b):
    mu = jnp.mean(z, axis=-1, keepdims=True)
    zc = z - mu
    var = jnp.mean(zc * zc, axis=-1, keepdims=True)
    return zc * lax.rsqrt(var + LN_EPS) * g + b


def _dot(a, b):
    return jnp.dot(a, b, preferred_element_type=F32)


def _dot_nt(a, b, precision=None):
    return lax.dot_general(a, b, (((1,), (1,)), ((), ())), precision=precision,
                           preferred_element_type=F32)


def _rope_angles(pos, half):
    inv = jnp.power(ROPE_THETA, -jnp.arange(half, dtype=F32) / half)
    return pos.astype(F32)[:, None] * inv[None, :]


def _rope_tables_split(pos, half):
    ang = _rope_angles(pos, half)
    return jnp.cos(ang), jnp.sin(ang)


def _rope_tables_rolled(pos, half, width=LANES):
    ang = _rope_angles(pos, half)
    cos = jnp.concatenate([jnp.cos(ang), jnp.cos(ang)], axis=-1)
    sin = jnp.concatenate([-jnp.sin(ang), jnp.sin(ang)], axis=-1)
    pad = width - 2 * half
    if pad:
        cos = jnp.concatenate([cos, jnp.ones((pos.shape[0], pad), F32)], axis=-1)
        sin = jnp.concatenate([sin, jnp.zeros((pos.shape[0], pad), F32)], axis=-1)
    return cos, sin


def _ret_proj_kernel(x_ref, w_ref, cos_ref, sin_ref, q_ref, k_ref, v_ref, g_ref):
    xb = x_ref[...].astype(BF16)
    cos = cos_ref[...]
    sin = sin_ref[...]
    hk = RET_HEADS * RET_KEY_DIM
    hv = RET_HEADS * RET_VAL_DIM
    half = RET_KEY_DIM // 2
    for h in range(RET_HEADS):
        for base, o_ref, scale in ((0, q_ref, None), (hk, k_ref, RET_KEY_DIM ** -0.5)):
            c0 = base + h * RET_KEY_DIM
            y = _dot(xb, w_ref[:, c0:c0 + RET_KEY_DIM])
            x1 = y[:, :half]
            x2 = y[:, half:]
            o1 = x1 * cos - x2 * sin
            o2 = x2 * cos + x1 * sin
            if scale is not None:
                o1 = o1 * scale
                o2 = o2 * scale
            o_ref[:, h * RET_KEY_DIM:h * RET_KEY_DIM + half] = o1.astype(o_ref.dtype)
            o_ref[:, h * RET_KEY_DIM + half:(h + 1) * RET_KEY_DIM] = o2.astype(o_ref.dtype)
    for h in range(RET_HEADS):
        c0 = h * RET_VAL_DIM
        v_ref[:, c0:c0 + RET_VAL_DIM] = _dot(
            xb, w_ref[:, 2 * hk + c0:2 * hk + c0 + RET_VAL_DIM]).astype(v_ref.dtype)
        g_ref[:, c0:c0 + RET_VAL_DIM] = _dot(
            xb, w_ref[:, 2 * hk + hv + c0:2 * hk + hv + c0 + RET_VAL_DIM]).astype(g_ref.dtype)


def _ret_proj(x, w, cos, sin, pos_tiles, out_dtype):
    m = x.shape[0]
    tm = _row_tile(m)
    hk = RET_HEADS * RET_KEY_DIM
    hv = RET_HEADS * RET_VAL_DIM
    half = RET_KEY_DIM // 2
    tab = pl.BlockSpec((tm, half), lambda i: (i % pos_tiles, 0))
    return pl.pallas_call(
        _ret_proj_kernel,
        grid=(m // tm,),
        in_specs=[pl.BlockSpec((tm, D_MODEL), lambda i: (i, 0)), _resident(w.shape), tab, tab],
        out_specs=[pl.BlockSpec((tm, hk), lambda i: (i, 0)), pl.BlockSpec((tm, hk), lambda i: (i, 0)),
                   pl.BlockSpec((tm, hv), lambda i: (i, 0)), pl.BlockSpec((tm, hv), lambda i: (i, 0))],
        out_shape=[jax.ShapeDtypeStruct((m, hk), out_dtype), jax.ShapeDtypeStruct((m, hk), out_dtype),
                   jax.ShapeDtypeStruct((m, hv), out_dtype), jax.ShapeDtypeStruct((m, hv), out_dtype)],
        compiler_params=_params("parallel"),
        name="ret_proj",
    )(x, w, cos, sin)


def _out_ln_kernel(a_ref, w_ref, x_ref, g_ref, b_ref, o_ref):
    y = _dot(a_ref[...].astype(BF16), w_ref[...])
    o_ref[...] = _layer_norm(DEEPNORM_ALPHA * x_ref[...] + y, g_ref[...], b_ref[...])


def _out_ln(a, w, x, g, b):
    m, ka = a.shape
    tm = _row_tile(m)
    return pl.pallas_call(
        _out_ln_kernel,
        grid=(m // tm,),
        in_specs=[pl.BlockSpec((tm, ka), lambda i: (i, 0)), _resident(w.shape),
                  pl.BlockSpec((tm, D_MODEL), lambda i: (i, 0)),
                  _resident((1, D_MODEL)), _resident((1, D_MODEL))],
        out_specs=pl.BlockSpec((tm, D_MODEL), lambda i: (i, 0)),
        out_shape=jax.ShapeDtypeStruct((m, D_MODEL), F32),
        compiler_params=_params("parallel"),
        name="out_ln",
    )(a, w, x, g, b)


FFN_CHUNK = 256


def _ffn_kernel(x_ref, w_in_ref, w_out_ref, g_ref, b_ref, o_ref, acc_ref):
    x = x_ref[...]
    xb = x.astype(BF16)
    for n, c in enumerate(range(0, FFN_DIM, FFN_CHUNK)):
        a = _dot(xb, w_in_ref[:, c:c + FFN_CHUNK])
        u = _dot(xb, w_in_ref[:, FFN_DIM + c:FFN_DIM + c + FFN_CHUNK])
        hact = (a * jax.nn.sigmoid(a) * u).astype(BF16)
        y = _dot(hact, w_out_ref[c:c + FFN_CHUNK, :])
        if n == 0:
            acc_ref[...] = y
        else:
            acc_ref[...] += y
    o_ref[...] = _layer_norm(DEEPNORM_ALPHA * x + acc_ref[...], g_ref[...], b_ref[...])


def _ffn_ln(x, w_in, w_out, g, b):
    m = x.shape[0]
    tm = _row_tile(m)
    return pl.pallas_call(
        _ffn_kernel,
        grid=(m // tm,),
        in_specs=[pl.BlockSpec((tm, D_MODEL), lambda i: (i, 0)), _resident(w_in.shape),
                  _resident(w_out.shape), _resident((1, D_MODEL)), _resident((1, D_MODEL))],
        out_specs=pl.BlockSpec((tm, D_MODEL), lambda i: (i, 0)),
        out_shape=jax.ShapeDtypeStruct((m, D_MODEL), F32),
        scratch_shapes=[pltpu.VMEM((tm, D_MODEL), F32)],
        compiler_params=_params("parallel"),
        name="ffn_ln",
    )(x, w_in, w_out, g, b)


def _group_norm_gate(o, norm_g, gate):
    mu = jnp.mean(o, axis=-1, keepdims=True)
    oc = o - mu
    var = jnp.mean(oc * oc, axis=-1, keepdims=True)
    on = oc * lax.rsqrt(var + LN_EPS) * norm_g
    return gate * jax.nn.sigmoid(gate) * on


def _ret_decay_tables(c):
    h = RET_HEADS
    lg = jnp.log1p(-jnp.exp2(-5.0 - jnp.arange(h, dtype=F32)))
    i = jnp.arange(c, dtype=F32)
    diff = i[:, None] - i[None, :]
    decay_in = jnp.where(diff[None] >= 0, jnp.exp(jnp.maximum(diff, 0.0)[None] * lg[:, None, None]), 0.0)
    q_dec = jnp.exp((i[None, :] + 1.0) * lg[:, None])
    k_dec = jnp.exp((c - 1.0 - i[None, :]) * lg[:, None])
    c_dec = jnp.exp(c * lg)
    dq = jnp.broadcast_to(q_dec[:, :, None], (h, c, RET_VAL_DIM))
    dk = jnp.broadcast_to(k_dec[:, :, None], (h, c, RET_VAL_DIM))
    dc = jnp.broadcast_to(c_dec[:, None, None], (h, 1, RET_VAL_DIM))
    return decay_in, dq, dk, dc


def _ret_core_kernel(q_ref, k_ref, v_ref, g_ref, din_ref, dq_ref, dk_ref, dc_ref, ng_ref,
                     o_ref, s_ref):
    c = pl.program_id(2)

    @pl.when(c == 0)
    def _():
        s_ref[...] = jnp.zeros_like(s_ref)

    q = q_ref[0]
    k = k_ref[0]
    v = v_ref[0]
    s = s_ref[0, 0]
    att = _dot_nt(q, k) * din_ref[0]
    inner = _dot(att.astype(BF16), v)
    cross = _dot(q, s.astype(BF16)) * dq_ref[0]
    vd = (v.astype(F32) * dk_ref[0]).astype(BF16)
    kt = k.astype(F32).T.astype(BF16)
    s_ref[0, 0] = s * dc_ref[0] + _dot(kt, vd)
    o_ref[0] = _group_norm_gate(inner + cross, ng_ref[...], g_ref[0].astype(F32)).astype(o_ref.dtype)


def _ret_core_prompt(q, k, v, g, norm_g):
    b, l, _ = q.shape
    c = math.gcd(l, RET_CHUNK)
    decay_in, dq, dk, dc = _ret_decay_tables(c)
    kspec = pl.BlockSpec((1, c, RET_KEY_DIM), lambda bi, h, ci: (bi, ci, h))
    vspec = pl.BlockSpec((1, c, RET_VAL_DIM), lambda bi, h, ci: (bi, ci, h))
    return pl.pallas_call(
        _ret_core_kernel,
        grid=(b, RET_HEADS, l // c),
        in_specs=[kspec, kspec, vspec, vspec,
                  pl.BlockSpec((1, c, c), lambda bi, h, ci: (h, 0, 0)),
                  pl.BlockSpec((1, c, RET_VAL_DIM), lambda bi, h, ci: (h, 0, 0)),
                  pl.BlockSpec((1, c, RET_VAL_DIM), lambda bi, h, ci: (h, 0, 0)),
                  pl.BlockSpec((1, 1, RET_VAL_DIM), lambda bi, h, ci: (h, 0, 0)),
                  pl.BlockSpec((1, RET_VAL_DIM), lambda bi, h, ci: (0, h))],
        out_specs=[vspec,
                   pl.BlockSpec((1, 1, RET_KEY_DIM, RET_VAL_DIM), lambda bi, h, ci: (bi, h, 0, 0))],
        out_shape=[jax.ShapeDtypeStruct((b, l, RET_HEADS * RET_VAL_DIM), BF16),
                   jax.ShapeDtypeStruct((b, RET_HEADS, RET_KEY_DIM, RET_VAL_DIM), F32)],
        compiler_params=_params("parallel", "parallel", "arbitrary"),
        name="ret_core",
    )(q, k, v, g, decay_in, dq, dk, dc, norm_g)


def _column(row, n):
    r = lax.broadcasted_iota(I32, (n, n), 0)
    c = lax.broadcasted_iota(I32, (n, n), 1)
    return jnp.sum(jnp.where(r == c, jnp.broadcast_to(row, (n, n)), 0.0), axis=1, keepdims=True)


def _ret_sample_kernel(q_ref, k_ref, v_ref, g_ref, s_ref, gam_ref, ng_ref, o_ref, so_ref):
    for h in range(RET_HEADS):
        ks = slice(h * RET_KEY_DIM, (h + 1) * RET_KEY_DIM)
        vs = slice(h * RET_VAL_DIM, (h + 1) * RET_VAL_DIM)
        qcol = _column(q_ref[0, :, ks], RET_KEY_DIM)
        kcol = _column(k_ref[0, :, ks], RET_KEY_DIM)
        v = v_ref[0, :, vs]
        s_new = s_ref[0, 0, h] * gam_ref[h] + kcol * v
        so_ref[0, 0, h] = s_new
        o = jnp.sum(qcol * s_new, axis=0, keepdims=True)
        o_ref[0, :, vs] = _group_norm_gate(o, ng_ref[:, vs], g_ref[0, :, vs])


def _ret_core_sample(q, k, v, g, state_ret, j, norm_g):
    db = q.shape[0]
    lg = jnp.log1p(-jnp.exp2(-5.0 - jnp.arange(RET_HEADS, dtype=F32)))
    gam = jnp.broadcast_to(jnp.exp(lg)[:, None, None], (RET_HEADS, 1, RET_VAL_DIM))
    kspec = pl.BlockSpec((1, 1, RET_HEADS * RET_KEY_DIM), lambda bi: (bi, 0, 0))
    vspec = pl.BlockSpec((1, 1, RET_HEADS * RET_VAL_DIM), lambda bi: (bi, 0, 0))
    return pl.pallas_call(
        _ret_sample_kernel,
        grid=(db,),
        in_specs=[kspec, kspec, vspec, vspec,
                  pl.BlockSpec((1, 1, RET_HEADS, RET_KEY_DIM, RET_VAL_DIM), lambda bi: (j, bi, 0, 0, 0)),
                  _resident(gam.shape), _resident(norm_g.shape)],
        out_specs=[vspec,
                   pl.BlockSpec((1, 1, RET_HEADS, RET_KEY_DIM, RET_VAL_DIM), lambda bi: (0, bi, 0, 0, 0))],
        out_shape=[jax.ShapeDtypeStruct((db, 1, RET_HEADS * RET_VAL_DIM), F32),
                   jax.ShapeDtypeStruct((1, db, RET_HEADS, RET_KEY_DIM, RET_VAL_DIM), F32)],
        compiler_params=_params("parallel"),
        name="ret_sample",
    )(q, k, v, g, state_ret, gam, norm_g)


POOL_HALO = 16


def _pool_prompt_kernel(x_ref, xh_ref, w_ref, sc_ref, g_ref, b_ref, o_ref, ext_ref, z_ref):
    i = pl.program_id(1)
    tm = x_ref.shape[1]
    x = x_ref[0]
    halo = jnp.where(i == 0, 0.0, xh_ref[0])
    ext_ref[0:POOL_HALO, :] = halo
    ext_ref[POOL_HALO:POOL_HALO + tm, :] = x
    pos = i * tm + lax.broadcasted_iota(I32, (tm, 1), 0)
    for gi, w in enumerate(POOL_WINDOWS):
        cs = slice(gi * POOL_GROUP_DIM, (gi + 1) * POOL_GROUP_DIM)
        wsum = ext_ref[POOL_HALO:POOL_HALO + tm, cs]
        for jj in range(1, w):
            wsum = wsum + ext_ref[POOL_HALO - jj:POOL_HALO - jj + tm, cs]
        cnt = jnp.minimum(w, pos + 1).astype(F32)
        d = wsum / cnt - x[:, cs]
        y = _dot(d.astype(BF16), w_ref[gi]) * sc_ref[:, cs]
        z_ref[:, cs] = DEEPNORM_ALPHA * x[:, cs] + y
    o_ref[0] = _layer_norm(z_ref[...], g_ref[...], b_ref[...])


def _pool_layer_prompt(x, w_grp, scale, g, b):
    bsz, l, _ = x.shape
    tm = _row_tile(l)
    hb = tm // POOL_HALO
    return pl.pallas_call(
        _pool_prompt_kernel,
        grid=(bsz, l // tm),
        in_specs=[pl.BlockSpec((1, tm, D_MODEL), lambda bi, i: (bi, i, 0)),
                  pl.BlockSpec((1, POOL_HALO, D_MODEL), lambda bi, i: (bi, jnp.maximum(i * hb - 1, 0), 0)),
                  _resident(w_grp.shape), _resident((1, D_MODEL)), _resident((1, D_MODEL)),
                  _resident((1, D_MODEL))],
        out_specs=pl.BlockSpec((1, tm, D_MODEL), lambda bi, i: (bi, i, 0)),
        out_shape=jax.ShapeDtypeStruct((bsz, l, D_MODEL), F32),
        scratch_shapes=[pltpu.VMEM((POOL_HALO + tm, D_MODEL), F32), pltpu.VMEM((tm, D_MODEL), F32)],
        compiler_params=_params("parallel", "arbitrary"),
        name="pool_prompt",
    )(x, x, w_grp, scale, g, b)


def _pool_sample_kernel(x_ref, h_ref, w_ref, sc_ref, g_ref, b_ref, o_ref, z_ref):
    x = x_ref[...]
    for gi, w in enumerate(POOL_WINDOWS):
        cs = slice(gi * POOL_GROUP_DIM, (gi + 1) * POOL_GROUP_DIM)
        wsum = x[:, cs]
        for jj in range(1, w):
            wsum = wsum + h_ref[POOL_HIST - jj, :, cs]
        d = wsum / float(w) - x[:, cs]
        y = _dot(d.astype(BF16), w_ref[gi]) * sc_ref[:, cs]
        z_ref[:, cs] = DEEPNORM_ALPHA * x[:, cs] + y
    o_ref[...] = _layer_norm(z_ref[...], g_ref[...], b_ref[...])


def _pool_layer_sample(x, hist_t, w_grp, scale, g, b):
    db = x.shape[0]
    return pl.pallas_call(
        _pool_sample_kernel,
        grid=(1,),
        in_specs=[_resident(x.shape), _resident(hist_t.shape), _resident(w_grp.shape),
                  _resident((1, D_MODEL)), _resident((1, D_MODEL)), _resident((1, D_MODEL))],
        out_specs=pl.BlockSpec((db, D_MODEL), lambda i: (0, 0)),
        out_shape=jax.ShapeDtypeStruct((db, D_MODEL), F32),
        scratch_shapes=[pltpu.VMEM((db, D_MODEL), F32)],
        compiler_params=_params("arbitrary"),
        name="pool_sample",
    )(x, hist_t, w_grp, scale, g, b)


DSA_QKV = 3 * D_MODEL
DSA_QI_PAD = IDX_HEADS * LANES
DSA_PAD_IN = DSA_QKV + DSA_QI_PAD + LANES
WI_LANE = IDX_DIM


def _dsa_pad_w_in(w_in):
    d = w_in.shape[0]
    qi0 = DSA_QKV
    ki0 = qi0 + IDX_HEADS * IDX_DIM
    z = jnp.zeros((d, LANES - IDX_DIM), w_in.dtype)
    parts = [w_in[:, :DSA_QKV]]
    for h in range(IDX_HEADS):
        parts += [w_in[:, qi0 + h * IDX_DIM:qi0 + (h + 1) * IDX_DIM], z]
    parts += [w_in[:, ki0:], jnp.zeros((d, LANES - IDX_DIM - IDX_HEADS), w_in.dtype)]
    return jnp.concatenate(parts, axis=1)


def _rope_rolled(y, cos, sin, half):
    if 2 * half == LANES:
        rot = pltpu.roll(y, half, 1)
    else:
        lane = lax.broadcasted_iota(I32, y.shape, 1)
        rot = jnp.where(lane < half, pltpu.roll(y, LANES - half, 1), pltpu.roll(y, half, 1))
    return y * cos + rot * sin


def _dsa_proj_kernel(x_ref, w_ref, cq_ref, sq_ref, ci_ref, si_ref,
                     qb_ref, k_ref, v_ref, kb_ref, vb_ref, qi_ref, kiw_ref):
    xb = x_ref[...].astype(BF16)
    cq = cq_ref[...]
    sq = sq_ref[...]
    ci = ci_ref[...]
    si = si_ref[...]
    hd = ATTN_HEAD_DIM
    for h in range(ATTN_HEADS):
        cs = slice(h * hd, (h + 1) * hd)
        yq = _dot(xb, w_ref[:, h * hd:(h + 1) * hd])
        qb_ref[:, cs] = _rope_rolled(yq, cq, sq, hd // 2).astype(BF16)
        yk = _rope_rolled(_dot(xb, w_ref[:, D_MODEL + h * hd:D_MODEL + (h + 1) * hd]), cq, sq, hd // 2)
        k_ref[:, cs] = yk
        kb_ref[:, cs] = yk.astype(BF16)
        yv = _dot(xb, w_ref[:, 2 * D_MODEL + h * hd:2 * D_MODEL + (h + 1) * hd])
        v_ref[:, cs] = yv
        vb_ref[:, cs] = yv.astype(BF16)
    for h in range(IDX_HEADS):
        c0 = DSA_QKV + h * LANES
        qi_ref[:, h * LANES:(h + 1) * LANES] = _rope_rolled(
            _dot(xb, w_ref[:, c0:c0 + LANES]), ci, si, IDX_DIM // 2)
    c0 = DSA_QKV + DSA_QI_PAD
    kiw_ref[...] = _rope_rolled(_dot(xb, w_ref[:, c0:c0 + LANES]), ci, si, IDX_DIM // 2)


def _dsa_proj(x, w_pad, tabs, pos_tiles):
    m = x.shape[0]
    tm = _row_tile(m)
    tab = pl.BlockSpec((tm, LANES), lambda i: (i % pos_tiles, 0))
    row = lambda n: pl.BlockSpec((tm, n), lambda i: (i, 0))
    sds = lambda n, dt: jax.ShapeDtypeStruct((m, n), dt)
    return pl.pallas_call(
        _dsa_proj_kernel,
        grid=(m // tm,),
        in_specs=[row(D_MODEL), _resident(w_pad.shape), tab, tab, tab, tab],
        out_specs=[row(D_MODEL), row(D_MODEL), row(D_MODEL), row(D_MODEL), row(D_MODEL),
                   row(DSA_QI_PAD), row(LANES)],
        out_shape=[sds(D_MODEL, BF16), sds(D_MODEL, F32), sds(D_MODEL, F32), sds(D_MODEL, BF16),
                   sds(D_MODEL, BF16), sds(DSA_QI_PAD, F32), sds(LANES, F32)],
        compiler_params=_params("parallel"),
        name="dsa_proj",
    )(x, w_pad, *tabs)


def _count(mask):
    return jnp.sum(jnp.where(mask, 1.0, 0.0), axis=-1, keepdims=True)


def _topk_bias(score, visible, col, n_top, idx_bits):
    bits = lax.bitcast_convert_type(score, I32)
    key = bits ^ ((bits >> 31) & 0x7FFFFFFF)
    key = jnp.where(visible, key, INT32_MIN)
    kf = float(n_top)
    rows = score.shape[0]
    zero = jnp.zeros((rows, 1), I32)
    t = jnp.where(_count(key >= zero) >= kf, zero, INT32_MIN)

    def value_bit(i, t):
        cand = t + jnp.left_shift(jnp.int32(1), 30 - i)
        return jnp.where(_count(key >= cand) >= kf, cand, t)

    t = lax.fori_loop(0, 31, value_bit, t)
    gt = key > t
    eq = key == t
    need = kf - _count(gt)

    def index_bit(i, j):
        cand = j + jnp.left_shift(jnp.int32(1), idx_bits - 1 - i)
        before = jnp.sum(jnp.where(eq, jnp.where(col < cand, 1.0, 0.0), 0.0), axis=-1, keepdims=True)
        return jnp.where(before < need, cand, j)

    j = lax.fori_loop(0, idx_bits, index_bit, zero)
    picked = jnp.where(gt, 0.0, jnp.where(eq, jnp.where(col <= j, 0.0, NEG_INF), NEG_INF))
    return jnp.where(visible, picked, NEG_INF)


def _indexer_scores(qi, wq, ki):
    acc = jnp.zeros((qi.shape[0], ki.shape[0]), F32)
    for h in range(IDX_HEADS):
        r = _dot_nt(qi[:, h * LANES:h * LANES + IDX_DIM], ki, precision=lax.Precision.HIGHEST)
        acc = acc + (wq[:, WI_LANE + h:WI_LANE + h + 1] * IDX_SCALE) * jnp.maximum(r, 0.0)
    return acc


def _dsa_attn_prompt_kernel(q_ref, qi_ref, kiwq_ref, kiw_ref, kb_ref, vb_ref, o_ref, *, n_top, idx_bits):
    i = pl.program_id(1)
    tq = q_ref.shape[1]
    n = kb_ref.shape[1]
    score = _indexer_scores(qi_ref[0], kiwq_ref[0], kiw_ref[0, :, :IDX_DIM])
    col = lax.broadcasted_iota(I32, (tq, n), 1)
    pos = i * tq + lax.broadcasted_iota(I32, (tq, n), 0)
    bias = _topk_bias(score, col <= pos, col, n_top, idx_bits)
    hd = ATTN_HEAD_DIM
    for h in range(ATTN_HEADS):
        cs = slice(h * hd, (h + 1) * hd)
        s = _dot_nt(q_ref[0, :, cs], kb_ref[0, :, cs]) * (hd ** -0.5) + bias
        m = jnp.max(s, axis=-1, keepdims=True)
        p = jnp.exp(s - m)
        l = jnp.sum(p, axis=-1, keepdims=True)
        o_ref[0, :, cs] = (_dot(p.astype(BF16), vb_ref[0, :, cs]) / l).astype(o_ref.dtype)


def _dsa_attn_prompt(qb, qi, kiw, kb, vb, n_top):
    b, l, _ = qb.shape
    tq = 256 if l % 256 == 0 else 128
    idx_bits = max(1, (l - 1).bit_length())
    qrow = lambda n: pl.BlockSpec((1, tq, n), lambda bi, i: (bi, i, 0))
    full = lambda n: pl.BlockSpec((1, l, n), lambda bi, i: (bi, 0, 0))
    return pl.pallas_call(
        functools.partial(_dsa_attn_prompt_kernel, n_top=n_top, idx_bits=idx_bits),
        grid=(b, l // tq),
        in_specs=[qrow(D_MODEL), qrow(DSA_QI_PAD), qrow(LANES), full(LANES), full(D_MODEL), full(D_MODEL)],
        out_specs=qrow(D_MODEL),
        out_shape=jax.ShapeDtypeStruct((b, l, D_MODEL), BF16),
        compiler_params=_params("parallel", "arbitrary"),
        name="dsa_attn_prompt",
    )(qb, qi, kiw, kiw, kb, vb)


def _dsa_idx_sample_kernel(pt_ref, qi_ref, kiwq_ref, *rest):
    del pt_ref
    pages, o_ref = rest[:-1], rest[-1]
    for p, page_ref in enumerate(pages):
        score = _indexer_scores(qi_ref[0], kiwq_ref[0], page_ref[0])
        o_ref[0, :, p * PAGE_SIZE:(p + 1) * PAGE_SIZE] = score


def _dsa_idx_sample(page_table, qi, kiw, cache_kidx):
    db, n_pages = page_table.shape
    qi8 = jnp.broadcast_to(qi[:, None, :], (db, 8, DSA_QI_PAD))
    kiw8 = jnp.broadcast_to(kiw[:, None, :], (db, 8, LANES))
    page_specs = [pl.BlockSpec((1, PAGE_SIZE, IDX_DIM), functools.partial(lambda p, bi, pt: (pt[bi, p], 0, 0), p))
                  for p in range(n_pages)]
    out = pl.pallas_call(
        _dsa_idx_sample_kernel,
        grid_spec=pltpu.PrefetchScalarGridSpec(
            num_scalar_prefetch=1,
            grid=(db,),
            in_specs=[pl.BlockSpec((1, 8, DSA_QI_PAD), lambda bi, pt: (bi, 0, 0)),
                      pl.BlockSpec((1, 8, LANES), lambda bi, pt: (bi, 0, 0))] + page_specs,
            out_specs=pl.BlockSpec((1, 8, n_pages * PAGE_SIZE), lambda bi, pt: (bi, 0, 0)),
        ),
        out_shape=jax.ShapeDtypeStruct((db, 8, n_pages * PAGE_SIZE), F32),
        compiler_params=_params("arbitrary"),
        name="dsa_idx_sample",
    )(page_table, qi8, kiw8, *([cache_kidx] * n_pages))
    return out[:, 0, :]


def _dsa_sel_sample_kernel(sp_ref, qi_ref, kiw_ref, o_ref, *, past, n_top, idx_bits):
    db = sp_ref.shape[0]
    kiw = kiw_ref[...]
    lane = lax.broadcasted_iota(I32, kiw.shape, 1)
    ki = jnp.where(lane < IDX_DIM, kiw, 0.0)
    qi = qi_ref[...]
    self_score = jnp.zeros((db, 1), F32)
    for h in range(IDX_HEADS):
        r = jnp.sum(qi[:, h * LANES:(h + 1) * LANES] * ki, axis=-1, keepdims=True)
        self_score = self_score + (kiw[:, WI_LANE + h:WI_LANE + h + 1] * IDX_SCALE) * jnp.maximum(r, 0.0)
    n = past + LANES
    score = jnp.concatenate([sp_ref[...], jnp.broadcast_to(self_score, (db, LANES))], axis=1)
    col = lax.broadcasted_iota(I32, (db, n), 1)
    o_ref[...] = _topk_bias(score, col <= past, col, n_top, idx_bits)


def _dsa_sel_sample(score_past, qi, kiw, n_top):
    db, past = score_past.shape
    n = past + LANES
    return pl.pallas_call(
        functools.partial(_dsa_sel_sample_kernel, past=past, n_top=n_top, idx_bits=n.bit_length()),
        grid=(1,),
        in_specs=[_resident(score_past.shape), _resident(qi.shape), _resident(kiw.shape)],
        out_specs=pl.BlockSpec((db, n), lambda i: (0, 0)),
        out_shape=jax.ShapeDtypeStruct((db, n), F32),
        compiler_params=_params("arbitrary"),
        name="dsa_sel_sample",
    )(score_past, qi, kiw)


def _dsa_attn_sample_kernel(pt_ref, q_ref, kn_ref, vn_ref, bias_ref, *rest, n_pages):
    del pt_ref
    kpages, vpages, o_ref = rest[:n_pages], rest[n_pages:2 * n_pages], rest[-1]
    hd = ATTN_HEAD_DIM
    lane_head = lax.broadcasted_iota(I32, (ATTN_HEADS, D_MODEL), 1) // hd
    head = lax.broadcasted_iota(I32, (ATTN_HEADS, D_MODEL), 0)
    own = lane_head == head
    qrows = jnp.where(own, jnp.broadcast_to(q_ref[0], (ATTN_HEADS, D_MODEL)), 0.0)
    qb = qrows.astype(BF16)
    scale = hd ** -0.5
    past = n_pages * PAGE_SIZE
    bias = bias_ref[0]
    s_pages = [_dot_nt(qb, kp[0].astype(BF16)) * scale + bias[:, p * PAGE_SIZE:(p + 1) * PAGE_SIZE]
               for p, kp in enumerate(kpages)]
    kn = kn_ref[0]
    s_self = jnp.sum(qb.astype(F32) * kn.astype(BF16).astype(F32), axis=-1, keepdims=True) * scale
    s_self = s_self + bias[:, past:past + 1]
    m = s_self
    for s in s_pages:
        m = jnp.maximum(m, jnp.max(s, axis=-1, keepdims=True))
    p_self = jnp.exp(s_self - m)
    l = p_self
    acc = p_self.astype(BF16).astype(F32) * vn_ref[0].astype(BF16).astype(F32)
    for s, vp in zip(s_pages, vpages):
        pexp = jnp.exp(s - m)
        l = l + jnp.sum(pexp, axis=-1, keepdims=True)
        acc = acc + _dot(pexp.astype(BF16), vp[0].astype(BF16))
    o_ref[0] = jnp.sum(jnp.where(own, acc / l, 0.0), axis=0, keepdims=True)


def _dsa_attn_sample(page_table, q, k_new, v_new, bias, cache_k, cache_v):
    db, n_pages = page_table.shape
    n = bias.shape[-1]
    row = pl.BlockSpec((1, 1, D_MODEL), lambda bi, pt: (bi, 0, 0))
    page_specs = [pl.BlockSpec((1, PAGE_SIZE, D_MODEL), functools.partial(lambda p, bi, pt: (pt[bi, p], 0, 0), p))
                  for p in range(n_pages)]
    return pl.pallas_call(
        functools.partial(_dsa_attn_sample_kernel, n_pages=n_pages),
        grid_spec=pltpu.PrefetchScalarGridSpec(
            num_scalar_prefetch=1,
            grid=(db,),
            in_specs=[row, row, row, pl.BlockSpec((1, 1, n), lambda bi, pt: (bi, 0, 0))] + page_specs + page_specs,
            out_specs=row,
        ),
        out_shape=jax.ShapeDtypeStruct((db, 1, D_MODEL), F32),
        compiler_params=_params("arbitrary"),
        name="dsa_attn_sample",
    )(page_table, q, k_new, v_new, bias, *([cache_k] * n_pages), *([cache_v] * n_pages))


def kernel(x_prompt, x_sample, state_ret, state_pool, cache_k, cache_v, cache_kidx, page_table,
           ret_w_in, ret_norm_g, ret_w_out, pool_w_grp, pool_scale, dsa_w_in, dsa_w_out,
           ffn_w_in, ffn_w_out, ln_g, ln_b):
    bp, lp, d = x_prompt.shape
    db, ls, _ = x_sample.shape
    assert d == D_MODEL and ls == 1
    past = page_table.shape[1] * PAGE_SIZE
    mp = bp * lp
    tmp = _row_tile(mp)
    assert lp % tmp == 0
    xp = x_prompt.reshape(mp, d)
    xs = x_sample.reshape(db, d)
    pos_p = jnp.arange(lp)
    pos_s = jnp.full((db,), past, jnp.int32)

    ret_p, ret_s, pool_p, pool_s = [], [], [], []
    kp_l, vp_l, kip_l, ks_l, vs_l, kis_l = [], [], [], [], [], []
    for i in range(DEPTH):
        kind, j = i % 3, i // 3
        g1, b1 = ln_g[i, 0][None, :], ln_b[i, 0][None, :]
        g2, b2 = ln_g[i, 1][None, :], ln_b[i, 1][None, :]
        if kind == 0:
            w_in = ret_w_in[j].astype(BF16)
            w_out = ret_w_out[j].astype(BF16)
            ng = ret_norm_g[j][None, :]
            half = RET_KEY_DIM // 2
            cos_p, sin_p = _rope_tables_split(pos_p, half)
            cos_s, sin_s = _rope_tables_split(pos_s, half)
            q, k, v, g = _ret_proj(xp, w_in, cos_p, sin_p, lp // tmp, BF16)
            shp = lambda a: a.reshape(bp, lp, a.shape[-1])
            o, sp = _ret_core_prompt(shp(q), shp(k), shp(v), shp(g), ng)
            xp = _out_ln(o.reshape(mp, -1), w_out, xp, g1, b1)
            q, k, v, g = _ret_proj(xs, w_in, cos_s, sin_s, 1, F32)
            o, ss = _ret_core_sample(q[:, None, :], k[:, None, :], v[:, None, :], g[:, None, :],
                                     state_ret, j, ng)
            xs = _out_ln(o.reshape(db, -1), w_out, xs, g1, b1)
            ret_p.append(sp)
            ret_s.append(ss[0])
        elif kind == 1:
            w_grp = pool_w_grp[j].astype(BF16)
            sc = pool_scale[j][None, :]
            pool_p.append(xp.reshape(bp, lp, d)[:, -POOL_HIST:])
            pool_s.append(jnp.concatenate([state_pool[j][:, 1:], xs[:, None, :]], axis=1))
            xp = _pool_layer_prompt(xp.reshape(bp, lp, d), w_grp, sc, g1, b1).reshape(mp, d)
            xs = _pool_layer_sample(xs, jnp.swapaxes(state_pool[j], 0, 1), w_grp, sc, g1, b1)
        else:
            w_pad = _dsa_pad_w_in(dsa_w_in[j]).astype(BF16)
            w_out = dsa_w_out[j].astype(BF16)
            tabs_p = _rope_tables_rolled(pos_p, ATTN_HEAD_DIM // 2) + _rope_tables_rolled(pos_p, IDX_DIM // 2)
            tabs_s = _rope_tables_rolled(pos_s, ATTN_HEAD_DIM // 2) + _rope_tables_rolled(pos_s, IDX_DIM // 2)
            qb, k32, v32, kb, vb, qi, kiw = _dsa_proj(xp, w_pad, tabs_p, lp // tmp)
            shp = lambda a: a.reshape(bp, lp, a.shape[-1])
            o = _dsa_attn_prompt(shp(qb), shp(qi), shp(kiw), shp(kb), shp(vb), min(TOPK_MAX, lp // 4))
            xp = _out_ln(o.reshape(mp, d), w_out, xp, g1, b1)
            kp_l.append(k32.reshape(bp, lp, ATTN_HEADS, ATTN_HEAD_DIM))
            vp_l.append(v32.reshape(bp, lp, ATTN_HEADS, ATTN_HEAD_DIM))
            kip_l.append(kiw[:, :IDX_DIM].reshape(bp, lp, IDX_DIM))

            qb, k32, v32, kb, vb, qi, kiw = _dsa_proj(xs, w_pad, tabs_s, 1)
            n_phys = cache_k.shape[1]
            score_past = _dsa_idx_sample(page_table, qi, kiw, cache_kidx[j])
            bias = _dsa_sel_sample(score_past, qi, kiw, min(TOPK_MAX, (past + ls) // 4))
            o = _dsa_attn_sample(page_table, qb.astype(F32)[:, None, :], k32[:, None, :], v32[:, None, :],
                                 bias[:, None, :],
                                 cache_k[j].reshape(n_phys, PAGE_SIZE, d), cache_v[j].reshape(n_phys, PAGE_SIZE, d))
            xs = _out_ln(o.reshape(db, d), w_out, xs, g1, b1)
            ks_l.append(k32.reshape(db, ls, ATTN_HEADS, ATTN_HEAD_DIM))
            vs_l.append(v32.reshape(db, ls, ATTN_HEADS, ATTN_HEAD_DIM))
            kis_l.append(kiw[:, :IDX_DIM].reshape(db, ls, IDX_DIM))
        w1 = ffn_w_in[i].astype(BF16)
        w2 = ffn_w_out[i].astype(BF16)
        xp = _ffn_ln(xp, w1, w2, g2, b2)
        xs = _ffn_ln(xs, w1, w2, g2, b2)
    return (xp.reshape(bp, lp, d), xs.reshape(db, ls, d), jnp.stack(ret_p), jnp.stack(ret_s),
            jnp.stack(pool_p), jnp.stack(pool_s), jnp.stack(kp_l), jnp.stack(vp_l), jnp.stack(kip_l),
            jnp.stack(ks_l), jnp.stack(vs_l), jnp.stack(kis_l))
```

```python
import functools
import math

import jax
import jax.numpy as jnp
from jax import lax
from jax.experimental import pallas as pl
from jax.experimental.pallas import tpu as pltpu

F32 = jnp.float32
BF16 = jnp.bfloat16
I32 = jnp.int32

LANES = 128
VMEM_LIMIT_BYTES = 56 * 1024 * 1024

D_MODEL = 1024
DEPTH = 4
PAGE_SIZE = 128
RET_HEADS = 4
RET_KEY_DIM = D_MODEL // RET_HEADS
RET_VAL_DIM = 2 * RET_KEY_DIM
RET_BLOCK = 256
POOL_WINDOWS = (2, 4, 8, 16)
POOL_GROUP_DIM = D_MODEL // len(POOL_WINDOWS)
POOL_HIST = max(POOL_WINDOWS) - 1
ATTN_HEADS = 8
ATTN_HEAD_DIM = D_MODEL // ATTN_HEADS
IDX_HEADS = 4
IDX_DIM = 64
IDX_SCALE = (IDX_HEADS ** -0.5) * (IDX_DIM ** -0.5)
TOPK_MAX = 256
FFN_DIM = -(-8 * D_MODEL // (3 * 256)) * 256
ROPE_THETA = 10000.0
LN_EPS = 1e-5
DEEPNORM_ALPHA = (2.0 * DEPTH) ** 0.25
INT32_MIN = -(2 ** 31)
NEG_INF = float("-inf")


def _params(*sem):
    return pltpu.CompilerParams(dimension_semantics=sem, vmem_limit_bytes=VMEM_LIMIT_BYTES)


def _resident(shape):
    nd = len(shape)
    return pl.BlockSpec(shape, lambda *_: (0,) * nd, pipeline_mode=pl.Buffered(1))


def _row_tile(m):
    for t in (512, 256, 128):
        if m % t == 0:
            return t
    raise ValueError(f"token count {m} is not a multiple of 128")


def _layer_norm(z, g, b):
    mu = jnp.mean(z, axis=-1, keepdims=True)
    zc = z - mu
    var = jnp.mean(zc * zc, axis=-1, keepdims=True)
    return zc * lax.rsqrt(var + LN_EPS) * g + b


def _dot(a, b):
    return jnp.dot(a, b, preferred_element_type=F32)


def _dot_nt(a, b, precision=None):
    return lax.dot_general(a, b, (((1,), (1,)), ((), ())), precision=precision,
                           preferred_element_type=F32)


def _rope_angles(pos, half):
    inv = jnp.power(ROPE_THETA, -jnp.arange(half, dtype=F32) / half)
    return pos.astype(F32)[:, None] * inv[None, :]


def _rope_tables_split(pos, half):
    ang = _rope_angles(pos, half)
    return jnp.cos(ang), jnp.sin(ang)


def _rope_tables_rolled(pos, half, width=LANES):
    ang = _rope_angles(pos, half)
    cos = jnp.concatenate([jnp.cos(ang), jnp.cos(ang)], axis=-1)
    sin = jnp.concatenate([-jnp.sin(ang), jnp.sin(ang)], axis=-1)
    pad = width - 2 * half
    if pad:
        cos = jnp.concatenate([cos, jnp.ones((pos.shape[0], pad), F32)], axis=-1)
        sin = jnp.concatenate([sin, jnp.zeros((pos.shape[0], pad), F32)], axis=-1)
    return cos, sin


def _ret_proj_kernel(x_ref, w_ref, cos_ref, sin_ref, q_ref, k_ref, v_ref, g_ref):
    xb = x_ref[...].astype(BF16)
    cos = cos_ref[...]
    sin = sin_ref[...]
    hk = RET_HEADS * RET_KEY_DIM
    hv = RET_HEADS * RET_VAL_DIM
    half = RET_KEY_DIM // 2
    for h in range(RET_HEADS):
        for base, o_ref, scale in ((0, q_ref, None), (hk, k_ref, RET_KEY_DIM ** -0.5)):
            c0 = base + h * RET_KEY_DIM
            y = _dot(xb, w_ref[:, c0:c0 + RET_KEY_DIM])
            x1 = y[:, :half]
            x2 = y[:, half:]
            o1 = x1 * cos - x2 * sin
            o2 = x2 * cos + x1 * sin
            if scale is not None:
                o1 = o1 * scale
                o2 = o2 * scale
            o_ref[:, h * RET_KEY_DIM:h * RET_KEY_DIM + half] = o1.astype(o_ref.dtype)
            o_ref[:, h * RET_KEY_DIM + half:(h + 1) * RET_KEY_DIM] = o2.astype(o_ref.dtype)
    for h in range(RET_HEADS):
        c0 = h * RET_VAL_DIM
        v_ref[:, c0:c0 + RET_VAL_DIM] = _dot(
            xb, w_ref[:, 2 * hk + c0:2 * hk + c0 + RET_VAL_DIM]).astype(v_ref.dtype)
        g_ref[:, c0:c0 + RET_VAL_DIM] = _dot(
            xb, w_ref[:, 2 * hk + hv + c0:2 * hk + hv + c0 + RET_VAL_DIM]).astype(g_ref.dtype)


def _ret_proj(x, w, cos, sin, pos_tiles, out_dtype):
    m = x.shape[0]
    tm = _row_tile(m)
    hk = RET_HEADS * RET_KEY_DIM
    hv = RET_HEADS * RET_VAL_DIM
    half = RET_KEY_DIM // 2
    tab = pl.BlockSpec((tm, half), lambda i: (i % pos_tiles, 0))
    return pl.pallas_call(
        _ret_proj_kernel,
        grid=(m // tm,),
        in_specs=[pl.BlockSpec((tm, D_MODEL), lambda i: (i, 0)), _resident(w.shape), tab, tab],
        out_specs=[pl.BlockSpec((tm, hk), lambda i: (i, 0)), pl.BlockSpec((tm, hk), lambda i: (i, 0)),
                   pl.BlockSpec((tm, hv), lambda i: (i, 0)), pl.BlockSpec((tm, hv), lambda i: (i, 0))],
        out_shape=[jax.ShapeDtypeStruct((m, hk), out_dtype), jax.ShapeDtypeStruct((m, hk), out_dtype),
                   jax.ShapeDtypeStruct((m, hv), out_dtype), jax.ShapeDtypeStruct((m, hv), out_dtype)],
        compiler_params=_params("parallel"),
        name="ret_proj",
    )(x, w, cos, sin)


def _out_ln_kernel(a_ref, w_ref, x_ref, g_ref, b_ref, o_ref):
    y = _dot(a_ref[...].astype(BF16), w_ref[...])
    o_ref[...] = _layer_norm(DEEPNORM_ALPHA * x_ref[...] + y, g_ref[...], b_ref[...])


def _out_ln(a, w, x, g, b):
    m, ka = a.shape
    tm = _row_tile(m)
    return pl.pallas_call(
        _out_ln_kernel,
        grid=(m // tm,),
        in_specs=[pl.BlockSpec((tm, ka), lambda i: (i, 0)), _resident(w.shape),
                  pl.BlockSpec((tm, D_MODEL), lambda i: (i, 0)),
                  _resident((1, D_MODEL)), _resident((1, D_MODEL))],
        out_specs=pl.BlockSpec((tm, D_MODEL), lambda i: (i, 0)),
        out_shape=jax.ShapeDtypeStruct((m, D_MODEL), F32),
        compiler_params=_params("parallel"),
        name="out_ln",
    )(a, w, x, g, b)


FFN_CHUNK = 256


def _ffn_kernel(x_ref, w_in_ref, w_out_ref, g_ref, b_ref, o_ref, acc_ref):
    x = x_ref[...]
    xb = x.astype(BF16)
    for n, c in enumerate(range(0, FFN_DIM, FFN_CHUNK)):
        a = _dot(xb, w_in_ref[:, c:c + FFN_CHUNK])
        u = _dot(xb, w_in_ref[:, FFN_DIM + c:FFN_DIM + c + FFN_CHUNK])
        hact = (a * jax.nn.sigmoid(a) * u).astype(BF16)
        y = _dot(hact, w_out_ref[c:c + FFN_CHUNK, :])
        if n == 0:
            acc_ref[...] = y
        else:
            acc_ref[...] += y
    o_ref[...] = _layer_norm(DEEPNORM_ALPHA * x + acc_ref[...], g_ref[...], b_ref[...])


def _ffn_ln(x, w_in, w_out, g, b):
    m = x.shape[0]
    tm = _row_tile(m)
    return pl.pallas_call(
        _ffn_kernel,
        grid=(m // tm,),
        in_specs=[pl.BlockSpec((tm, D_MODEL), lambda i: (i, 0)), _resident(w_in.shape),
                  _resident(w_out.shape), _resident((1, D_MODEL)), _resident((1, D_MODEL))],
        out_specs=pl.BlockSpec((tm, D_MODEL), lambda i: (i, 0)),
        out_shape=jax.ShapeDtypeStruct((m, D_MODEL), F32),
        scratch_shapes=[pltpu.VMEM((tm, D_MODEL), F32)],
        compiler_params=_params("parallel"),
        name="ffn_ln",
    )(x, w_in, w_out, g, b)


def _group_norm_gate(o, norm_g, gate):
    mu = jnp.mean(o, axis=-1, keepdims=True)
    oc = o - mu
    var = jnp.mean(oc * oc, axis=-1, keepdims=True)
    on = oc * lax.rsqrt(var + LN_EPS) * norm_g
    return gate * jax.nn.sigmoid(gate) * on


def _ret_decay_tables(c):
    h = RET_HEADS
    lg = jnp.log1p(-jnp.exp2(-5.0 - jnp.arange(h, dtype=F32)))
    i = jnp.arange(c, dtype=F32)
    diff = i[:, None] - i[None, :]
    decay_in = jnp.where(diff[None] >= 0, jnp.exp(jnp.maximum(diff, 0.0)[None] * lg[:, None, None]), 0.0)
    q_dec = jnp.exp((i[None, :] + 1.0) * lg[:, None])
    k_dec = jnp.exp((c - 1.0 - i[None, :]) * lg[:, None])
    c_dec = jnp.exp(c * lg)
    dq = jnp.broadcast_to(q_dec[:, :, None], (h, c, RET_VAL_DIM))
    dk = jnp.broadcast_to(k_dec[:, :, None], (h, c, RET_VAL_DIM))
    dc = jnp.broadcast_to(c_dec[:, None, None], (h, 1, RET_VAL_DIM))
    return decay_in, dq, dk, dc


def _ret_core_kernel(q_ref, k_ref, v_ref, g_ref, din_ref, dq_ref, dk_ref, dc_ref, ng_ref,
                     o_ref, s_ref):
    c = pl.program_id(1)

    @pl.when(c == 0)
    def _():
        s_ref[...] = jnp.zeros_like(s_ref)

    for h in range(RET_HEADS):
        ks = slice(h * RET_KEY_DIM, (h + 1) * RET_KEY_DIM)
        vs = slice(h * RET_VAL_DIM, (h + 1) * RET_VAL_DIM)
        q = q_ref[0, :, ks]
        k = k_ref[0, :, ks]
        v = v_ref[0, :, vs]
        s = s_ref[0, h]
        att = _dot_nt(q, k) * din_ref[h]
        inner = _dot(att.astype(BF16), v)
        cross = _dot(q, s.astype(BF16)) * dq_ref[h]
        vd = (v.astype(F32) * dk_ref[h]).astype(BF16)
        kt = k.astype(F32).T.astype(BF16)
        s_ref[0, h] = s * dc_ref[h] + _dot(kt, vd)
        o_ref[0, :, vs] = _group_norm_gate(inner + cross, ng_ref[:, vs],
                                           g_ref[0, :, vs].astype(F32)).astype(o_ref.dtype)


def _ret_core_prompt(q, k, v, g, norm_g):
    b, l, _ = q.shape
    c = math.gcd(l, RET_BLOCK)
    hk = RET_HEADS * RET_KEY_DIM
    hv = RET_HEADS * RET_VAL_DIM
    decay_in, dq, dk, dc = _ret_decay_tables(c)
    kspec = pl.BlockSpec((1, c, hk), lambda bi, ci: (bi, ci, 0))
    vspec = pl.BlockSpec((1, c, hv), lambda bi, ci: (bi, ci, 0))
    return pl.pallas_call(
        _ret_core_kernel,
        grid=(b, l // c),
        in_specs=[kspec, kspec, vspec, vspec, _resident(decay_in.shape), _resident(dq.shape),
                  _resident(dk.shape), _resident(dc.shape), _resident(norm_g.shape)],
        out_specs=[vspec,
                   pl.BlockSpec((1, RET_HEADS, RET_KEY_DIM, RET_VAL_DIM), lambda bi, ci: (bi, 0, 0, 0))],
        out_shape=[jax.ShapeDtypeStruct((b, l, hv), BF16),
                   jax.ShapeDtypeStruct((b, RET_HEADS, RET_KEY_DIM, RET_VAL_DIM), F32)],
        compiler_params=_params("parallel", "arbitrary"),
        name="ret_core",
    )(q, k, v, g, decay_in, dq, dk, dc, norm_g)


def _column(row, n):
    r = lax.broadcasted_iota(I32, (n, n), 0)
    c = lax.broadcasted_iota(I32, (n, n), 1)
    return jnp.sum(jnp.where(r == c, jnp.broadcast_to(row, (n, n)), 0.0), axis=1, keepdims=True)


def _ret_sample_kernel(q_ref, k_ref, v_ref, g_ref, s_ref, gam_ref, ng_ref, *rest):
    o_ref, so_ref = rest[-2:]
    for h in range(RET_HEADS):
        ks = slice(h * RET_KEY_DIM, (h + 1) * RET_KEY_DIM)
        vs = slice(h * RET_VAL_DIM, (h + 1) * RET_VAL_DIM)
        qcol = _column(q_ref[0, :, ks], RET_KEY_DIM)
        kcol = _column(k_ref[0, :, ks], RET_KEY_DIM)
        v = v_ref[0, :, vs]
        s_new = s_ref[0, 0, h] * gam_ref[h] + kcol * v
        so_ref[0, 0, h] = s_new
        o = jnp.sum(qcol * s_new, axis=0, keepdims=True)
        o_ref[0, :, vs] = _group_norm_gate(o, ng_ref[:, vs], g_ref[0, :, vs])


def _ret_core_sample(q, k, v, g, state_ret, j, norm_g, new_states):
    db = q.shape[0]
    lg = jnp.log1p(-jnp.exp2(-5.0 - jnp.arange(RET_HEADS, dtype=F32)))
    gam = jnp.broadcast_to(jnp.exp(lg)[:, None, None], (RET_HEADS, 1, RET_VAL_DIM))
    kspec = pl.BlockSpec((1, 1, RET_HEADS * RET_KEY_DIM), lambda bi: (bi, 0, 0))
    vspec = pl.BlockSpec((1, 1, RET_HEADS * RET_VAL_DIM), lambda bi: (bi, 0, 0))
    sspec = pl.BlockSpec((1, 1, RET_HEADS, RET_KEY_DIM, RET_VAL_DIM), lambda bi: (j, bi, 0, 0, 0))
    in_specs = [kspec, kspec, vspec, vspec, sspec, _resident(gam.shape), _resident(norm_g.shape)]
    args = [q, k, v, g, state_ret, gam, norm_g]
    aliases = {}
    if new_states is not None:
        in_specs.append(pl.BlockSpec(memory_space=pl.ANY))
        args.append(new_states)
        aliases = {len(args) - 1: 1}
    return pl.pallas_call(
        _ret_sample_kernel,
        grid=(db,),
        in_specs=in_specs,
        out_specs=[vspec, sspec],
        out_shape=[jax.ShapeDtypeStruct((db, 1, RET_HEADS * RET_VAL_DIM), F32),
                   jax.ShapeDtypeStruct(state_ret.shape, F32)],
        input_output_aliases=aliases,
        compiler_params=_params("parallel"),
        name="ret_sample",
    )(*args)


POOL_HALO = 16


def _pool_prompt_kernel(x_ref, xh_ref, w_ref, sc_ref, g_ref, b_ref, o_ref, ext_ref, z_ref):
    i = pl.program_id(1)
    tm = x_ref.shape[1]
    x = x_ref[0]
    halo = jnp.where(i == 0, 0.0, xh_ref[0])
    ext_ref[0:POOL_HALO, :] = halo
    ext_ref[POOL_HALO:POOL_HALO + tm, :] = x
    pos = i * tm + lax.broadcasted_iota(I32, (tm, 1), 0)
    for gi, w in enumerate(POOL_WINDOWS):
        cs = slice(gi * POOL_GROUP_DIM, (gi + 1) * POOL_GROUP_DIM)
        wsum = ext_ref[POOL_HALO:POOL_HALO + tm, cs]
        for jj in range(1, w):
            wsum = wsum + ext_ref[POOL_HALO - jj:POOL_HALO - jj + tm, cs]
        cnt = jnp.minimum(w, pos + 1).astype(F32)
        d = wsum / cnt - x[:, cs]
        y = _dot(d.astype(BF16), w_ref[gi]) * sc_ref[:, cs]
        z_ref[:, cs] = DEEPNORM_ALPHA * x[:, cs] + y
    o_ref[0] = _layer_norm(z_ref[...], g_ref[...], b_ref[...])


def _pool_layer_prompt(x, w_grp, scale, g, b):
    bsz, l, _ = x.shape
    tm = _row_tile(l)
    hb = tm // POOL_HALO
    return pl.pallas_call(
        _pool_prompt_kernel,
        grid=(bsz, l // tm),
        in_specs=[pl.BlockSpec((1, tm, D_MODEL), lambda bi, i: (bi, i, 0)),
                  pl.BlockSpec((1, POOL_HALO, D_MODEL), lambda bi, i: (bi, jnp.maximum(i * hb - 1, 0), 0)),
                  _resident(w_grp.shape), _resident((1, D_MODEL)), _resident((1, D_MODEL)),
                  _resident((1, D_MODEL))],
        out_specs=pl.BlockSpec((1, tm, D_MODEL), lambda bi, i: (bi, i, 0)),
        out_shape=jax.ShapeDtypeStruct((bsz, l, D_MODEL), F32),
        scratch_shapes=[pltpu.VMEM((POOL_HALO + tm, D_MODEL), F32), pltpu.VMEM((tm, D_MODEL), F32)],
        compiler_params=_params("parallel", "arbitrary"),
        name="pool_prompt",
    )(x, x, w_grp, scale, g, b)


def _pool_sample_kernel(x_ref, h_ref, w_ref, sc_ref, g_ref, b_ref, o_ref, z_ref):
    x = x_ref[...]
    for gi, w in enumerate(POOL_WINDOWS):
        cs = slice(gi * POOL_GROUP_DIM, (gi + 1) * POOL_GROUP_DIM)
        wsum = x[:, cs]
        for jj in range(1, w):
            wsum = wsum + h_ref[POOL_HIST - jj, :, cs]
        d = wsum / float(w) - x[:, cs]
        y = _dot(d.astype(BF16), w_ref[gi]) * sc_ref[:, cs]
        z_ref[:, cs] = DEEPNORM_ALPHA * x[:, cs] + y
    o_ref[...] = _layer_norm(z_ref[...], g_ref[...], b_ref[...])


def _pool_layer_sample(x, hist_t, w_grp, scale, g, b):
    db = x.shape[0]
    return pl.pallas_call(
        _pool_sample_kernel,
        grid=(1,),
        in_specs=[_resident(x.shape), _resident(hist_t.shape), _resident(w_grp.shape),
                  _resident((1, D_MODEL)), _resident((1, D_MODEL)), _resident((1, D_MODEL))],
        out_specs=pl.BlockSpec((db, D_MODEL), lambda i: (0, 0)),
        out_shape=jax.ShapeDtypeStruct((db, D_MODEL), F32),
        scratch_shapes=[pltpu.VMEM((db, D_MODEL), F32)],
        compiler_params=_params("arbitrary"),
        name="pool_sample",
    )(x, hist_t, w_grp, scale, g, b)


DSA_QKV = 3 * D_MODEL
DSA_QI_PAD = IDX_HEADS * LANES
DSA_PAD_IN = DSA_QKV + DSA_QI_PAD + LANES
WI_LANE = IDX_DIM


def _dsa_pad_w_in(w_in):
    d = w_in.shape[0]
    qi0 = DSA_QKV
    ki0 = qi0 + IDX_HEADS * IDX_DIM
    z = jnp.zeros((d, LANES - IDX_DIM), w_in.dtype)
    parts = [w_in[:, :DSA_QKV]]
    for h in range(IDX_HEADS):
        parts += [w_in[:, qi0 + h * IDX_DIM:qi0 + (h + 1) * IDX_DIM], z]
    parts += [w_in[:, ki0:], jnp.zeros((d, LANES - IDX_DIM - IDX_HEADS), w_in.dtype)]
    return jnp.concatenate(parts, axis=1)


def _rope_rolled(y, cos, sin, half):
    if 2 * half == LANES:
        rot = pltpu.roll(y, half, 1)
    else:
        lane = lax.broadcasted_iota(I32, y.shape, 1)
        rot = jnp.where(lane < half, pltpu.roll(y, LANES - half, 1), pltpu.roll(y, half, 1))
    return y * cos + rot * sin


def _dsa_proj_kernel(x_ref, w_ref, cq_ref, sq_ref, ci_ref, si_ref,
                     qb_ref, k_ref, v_ref, kb_ref, vb_ref, qi_ref, kiw_ref):
    xb = x_ref[...].astype(BF16)
    cq = cq_ref[...]
    sq = sq_ref[...]
    ci = ci_ref[...]
    si = si_ref[...]
    hd = ATTN_HEAD_DIM
    for h in range(ATTN_HEADS):
        cs = slice(h * hd, (h + 1) * hd)
        yq = _dot(xb, w_ref[:, h * hd:(h + 1) * hd])
        qb_ref[:, cs] = _rope_rolled(yq, cq, sq, hd // 2).astype(BF16)
        yk = _rope_rolled(_dot(xb, w_ref[:, D_MODEL + h * hd:D_MODEL + (h + 1) * hd]), cq, sq, hd // 2)
        k_ref[:, cs] = yk
        kb_ref[:, cs] = yk.astype(BF16)
        yv = _dot(xb, w_ref[:, 2 * D_MODEL + h * hd:2 * D_MODEL + (h + 1) * hd])
        v_ref[:, cs] = yv
        vb_ref[:, cs] = yv.astype(BF16)
    for h in range(IDX_HEADS):
        c0 = DSA_QKV + h * LANES
        qi_ref[:, h * LANES:(h + 1) * LANES] = _rope_rolled(
            _dot(xb, w_ref[:, c0:c0 + LANES]), ci, si, IDX_DIM // 2)
    c0 = DSA_QKV + DSA_QI_PAD
    kiw_ref[...] = _rope_rolled(_dot(xb, w_ref[:, c0:c0 + LANES]), ci, si, IDX_DIM // 2)


def _dsa_proj(x, w_pad, tabs, pos_tiles):
    m = x.shape[0]
    tm = _row_tile(m)
    tab = pl.BlockSpec((tm, LANES), lambda i: (i % pos_tiles, 0))
    row = lambda n: pl.BlockSpec((tm, n), lambda i: (i, 0))
    sds = lambda n, dt: jax.ShapeDtypeStruct((m, n), dt)
    return pl.pallas_call(
        _dsa_proj_kernel,
        grid=(m // tm,),
        in_specs=[row(D_MODEL), _resident(w_pad.shape), tab, tab, tab, tab],
        out_specs=[row(D_MODEL), row(D_MODEL), row(D_MODEL), row(D_MODEL), row(D_MODEL),
                   row(DSA_QI_PAD), row(LANES)],
        out_shape=[sds(D_MODEL, BF16), sds(D_MODEL, F32), sds(D_MODEL, F32), sds(D_MODEL, BF16),
                   sds(D_MODEL, BF16), sds(DSA_QI_PAD, F32), sds(LANES, F32)],
        compiler_params=_params("parallel"),
        name="dsa_proj",
    )(x, w_pad, *tabs)


def _count(mask):
    return jnp.sum(jnp.where(mask, 1.0, 0.0), axis=-1, keepdims=True)


def _topk_bias(score, visible, col, n_top, idx_bits):
    bits = lax.bitcast_convert_type(score, I32)
    key = bits ^ ((bits >> 31) & 0x7FFFFFFF)
    key = jnp.where(visible, key, INT32_MIN)
    kf = float(n_top)
    rows = score.shape[0]
    zero = jnp.zeros((rows, 1), I32)
    t = jnp.where(_count(key >= zero) >= kf, zero, INT32_MIN)

    def value_bit(i, t):
        cand = t + jnp.left_shift(jnp.int32(1), 30 - i)
        return jnp.where(_count(key >= cand) >= kf, cand, t)

    t = lax.fori_loop(0, 31, value_bit, t)
    gt = key > t
    eq = key == t
    need = kf - _count(gt)

    def index_bit(i, j):
        cand = j + jnp.left_shift(jnp.int32(1), idx_bits - 1 - i)
        before = jnp.sum(jnp.where(eq, jnp.where(col < cand, 1.0, 0.0), 0.0), axis=-1, keepdims=True)
        return jnp.where(before < need, cand, j)

    j = lax.fori_loop(0, idx_bits, index_bit, zero)
    picked = jnp.where(gt, 0.0, jnp.where(eq, jnp.where(col <= j, 0.0, NEG_INF), NEG_INF))
    return jnp.where(visible, picked, NEG_INF)


def _stack_idx_heads(qi):
    return jnp.concatenate([qi[:, h * LANES:h * LANES + IDX_DIM] for h in range(IDX_HEADS)],
                           axis=0).astype(BF16)


def _indexer_scores(qs, wq, ki):
    rows = wq.shape[0]
    r = _dot_nt(qs, ki.astype(BF16))
    acc = jnp.zeros((rows, ki.shape[0]), F32)
    for h in range(IDX_HEADS):
        acc = acc + (wq[:, WI_LANE + h:WI_LANE + h + 1] * IDX_SCALE) * jnp.maximum(
            r[h * rows:(h + 1) * rows], 0.0)
    return acc


def _dsa_attn_prompt_kernel(q_ref, qi_ref, kiwq_ref, kiw_ref, kb_ref, vb_ref, o_ref, *, n_top, widths):
    i = pl.program_id(1)
    tq = q_ref.shape[1]
    hd = ATTN_HEAD_DIM

    def attend(n):
        score = _indexer_scores(_stack_idx_heads(qi_ref[0]), kiwq_ref[0], kiw_ref[0, :n, :IDX_DIM])
        col = lax.broadcasted_iota(I32, (tq, n), 1)
        pos = i * tq + lax.broadcasted_iota(I32, (tq, n), 0)
        bias = _topk_bias(score, col <= pos, col, n_top, max(1, (n - 1).bit_length()))
        for h in range(ATTN_HEADS):
            cs = slice(h * hd, (h + 1) * hd)
            s = _dot_nt(q_ref[0, :, cs], kb_ref[0, :n, cs]) * (hd ** -0.5) + bias
            m = jnp.max(s, axis=-1, keepdims=True)
            p = jnp.exp(s - m)
            l = jnp.sum(p, axis=-1, keepdims=True)
            o_ref[0, :, cs] = (_dot(p.astype(BF16), vb_ref[0, :n, cs]) / l).astype(o_ref.dtype)

    last = (i + 1) * tq
    prev = 0
    for n in widths:
        pl.when(jnp.logical_and(last > prev, last <= n))(functools.partial(attend, n))
        prev = n


def _dsa_attn_prompt(qb, qi, kiw, kb, vb, n_top):
    b, l, _ = qb.shape
    tq = 256 if l % 256 == 0 else 128
    step = 2 * tq if l % (2 * tq) == 0 else tq
    widths = tuple(range(step, l + 1, step))
    qrow = lambda n: pl.BlockSpec((1, tq, n), lambda bi, i: (bi, i, 0))
    full = lambda n: pl.BlockSpec((1, l, n), lambda bi, i: (bi, 0, 0))
    return pl.pallas_call(
        functools.partial(_dsa_attn_prompt_kernel, n_top=n_top, widths=widths),
        grid=(b, l // tq),
        in_specs=[qrow(D_MODEL), qrow(DSA_QI_PAD), qrow(LANES), full(LANES), full(D_MODEL), full(D_MODEL)],
        out_specs=qrow(D_MODEL),
        out_shape=jax.ShapeDtypeStruct((b, l, D_MODEL), BF16),
        compiler_params=_params("parallel", "arbitrary"),
        name="dsa_attn_prompt",
    )(qb, qi, kiw, kiw, kb, vb)


def _dsa_idx_sample_kernel(pt_ref, qi_ref, kiwq_ref, *rest):
    del pt_ref
    pages, o_ref = rest[:-1], rest[-1]
    qs = _stack_idx_heads(qi_ref[0])
    wq = kiwq_ref[0]
    for p, page_ref in enumerate(pages):
        o_ref[0, :, p * PAGE_SIZE:(p + 1) * PAGE_SIZE] = _indexer_scores(qs, wq, page_ref[0])


def _dsa_idx_sample(page_table, qi, kiw, cache_kidx):
    db, n_pages = page_table.shape
    qi8 = jnp.broadcast_to(qi[:, None, :], (db, 8, DSA_QI_PAD))
    kiw8 = jnp.broadcast_to(kiw[:, None, :], (db, 8, LANES))
    page_specs = [pl.BlockSpec((1, PAGE_SIZE, IDX_DIM), functools.partial(lambda p, bi, pt: (pt[bi, p], 0, 0), p))
                  for p in range(n_pages)]
    out = pl.pallas_call(
        _dsa_idx_sample_kernel,
        grid_spec=pltpu.PrefetchScalarGridSpec(
            num_scalar_prefetch=1,
            grid=(db,),
            in_specs=[pl.BlockSpec((1, 8, DSA_QI_PAD), lambda bi, pt: (bi, 0, 0)),
                      pl.BlockSpec((1, 8, LANES), lambda bi, pt: (bi, 0, 0))] + page_specs,
            out_specs=pl.BlockSpec((1, 8, n_pages * PAGE_SIZE), lambda bi, pt: (bi, 0, 0)),
        ),
        out_shape=jax.ShapeDtypeStruct((db, 8, n_pages * PAGE_SIZE), F32),
        compiler_params=_params("arbitrary"),
        name="dsa_idx_sample",
    )(page_table, qi8, kiw8, *([cache_kidx] * n_pages))
    return out[:, 0, :]


def _dsa_sel_sample_kernel(sp_ref, qi_ref, kiw_ref, o_ref, *, past, n_top, idx_bits):
    db = sp_ref.shape[0]
    kiw = kiw_ref[...]
    lane = lax.broadcasted_iota(I32, kiw.shape, 1)
    ki = jnp.where(lane < IDX_DIM, kiw, 0.0)
    qi = qi_ref[...]
    self_score = jnp.zeros((db, 1), F32)
    for h in range(IDX_HEADS):
        r = jnp.sum(qi[:, h * LANES:(h + 1) * LANES] * ki, axis=-1, keepdims=True)
        self_score = self_score + (kiw[:, WI_LANE + h:WI_LANE + h + 1] * IDX_SCALE) * jnp.maximum(r, 0.0)
    n = past + LANES
    score = jnp.concatenate([sp_ref[...], jnp.broadcast_to(self_score, (db, LANES))], axis=1)
    col = lax.broadcasted_iota(I32, (db, n), 1)
    o_ref[...] = _topk_bias(score, col <= past, col, n_top, idx_bits)


def _dsa_sel_sample(score_past, qi, kiw, n_top):
    db, past = score_past.shape
    n = past + LANES
    return pl.pallas_call(
        functools.partial(_dsa_sel_sample_kernel, past=past, n_top=n_top, idx_bits=n.bit_length()),
        grid=(1,),
        in_specs=[_resident(score_past.shape), _resident(qi.shape), _resident(kiw.shape)],
        out_specs=pl.BlockSpec((db, n), lambda i: (0, 0)),
        out_shape=jax.ShapeDtypeStruct((db, n), F32),
        compiler_params=_params("arbitrary"),
        name="dsa_sel_sample",
    )(score_past, qi, kiw)


def _dsa_attn_sample_kernel(pt_ref, q_ref, kn_ref, vn_ref, bias_ref, bself_ref, *rest, n_pages):
    del pt_ref
    kpages, vpages, o_ref = rest[:n_pages], rest[n_pages:2 * n_pages], rest[-1]
    hd = ATTN_HEAD_DIM
    rows = PAGE_SIZE * ATTN_HEADS
    qb = q_ref[0]
    scale = hd ** -0.5
    head = lax.broadcasted_iota(I32, (ATTN_HEADS, rows), 0)
    col = lax.broadcasted_iota(I32, (ATTN_HEADS, rows), 1)
    own = jnp.bitwise_and(col, ATTN_HEADS - 1) == head
    s_pages = []
    for p, kp in enumerate(kpages):
        k2 = kp[0].reshape(rows, hd).astype(BF16)
        s = _dot_nt(qb, k2) * scale + bias_ref[0, :, p * rows:(p + 1) * rows]
        s_pages.append(jnp.where(own, s, NEG_INF))
    s_self = jnp.sum(qb.astype(F32) * kn_ref[0].astype(BF16).astype(F32), axis=-1, keepdims=True) * scale
    s_self = s_self + bself_ref[0, :, 0:1]
    m = s_self
    for s in s_pages:
        m = jnp.maximum(m, jnp.max(s, axis=-1, keepdims=True))
    p_self = jnp.exp(s_self - m)
    l = p_self
    acc = p_self.astype(BF16).astype(F32) * vn_ref[0].astype(BF16).astype(F32)
    for s, vp in zip(s_pages, vpages):
        pexp = jnp.exp(s - m)
        l = l + jnp.sum(pexp, axis=-1, keepdims=True)
        acc = acc + _dot(pexp.astype(BF16), vp[0].reshape(rows, hd).astype(BF16))
    o_ref[0] = acc / l


def _dsa_attn_sample(page_table, q, k_new, v_new, bias, cache_k, cache_v):
    db, n_pages = page_table.shape
    past = n_pages * PAGE_SIZE
    bias_keys = jnp.repeat(bias[:, :past], ATTN_HEADS, axis=1)[:, None, :]
    bias_self = bias[:, None, past:]
    row = pl.BlockSpec((1, ATTN_HEADS, ATTN_HEAD_DIM), lambda bi, pt: (bi, 0, 0))
    page_spec = lambda p: pl.BlockSpec((1, PAGE_SIZE, ATTN_HEADS, ATTN_HEAD_DIM),
                                       functools.partial(lambda p, bi, pt: (pt[bi, p], 0, 0, 0), p))
    return pl.pallas_call(
        functools.partial(_dsa_attn_sample_kernel, n_pages=n_pages),
        grid_spec=pltpu.PrefetchScalarGridSpec(
            num_scalar_prefetch=1,
            grid=(db,),
            in_specs=[row, row, row,
                      pl.BlockSpec((1, 1, past * ATTN_HEADS), lambda bi, pt: (bi, 0, 0)),
                      pl.BlockSpec((1, 1, LANES), lambda bi, pt: (bi, 0, 0))]
                     + [page_spec(p) for p in range(n_pages)] + [page_spec(p) for p in range(n_pages)],
            out_specs=row,
        ),
        out_shape=jax.ShapeDtypeStruct((db, ATTN_HEADS, ATTN_HEAD_DIM), F32),
        compiler_params=_params("arbitrary"),
        name="dsa_attn_sample",
    )(page_table, q, k_new, v_new, bias_keys, bias_self, *([cache_k] * n_pages), *([cache_v] * n_pages))


def kernel(x_prompt, x_sample, state_ret, state_pool, cache_k, cache_v, cache_kidx, page_table,
           ret_w_in, ret_norm_g, ret_w_out, pool_w_grp, pool_scale, dsa_w_in, dsa_w_out,
           ffn_w_in, ffn_w_out, ln_g, ln_b):
    bp, lp, d = x_prompt.shape
    db, ls, _ = x_sample.shape
    assert d == D_MODEL and ls == 1
    past = page_table.shape[1] * PAGE_SIZE
    mp = bp * lp
    tmp = _row_tile(mp)
    assert lp % tmp == 0
    xp = x_prompt.reshape(mp, d)
    xs = x_sample.reshape(db, d)
    pos_p = jnp.arange(lp)
    pos_s = jnp.full((db,), past, jnp.int32)

    ret_p, pool_p, pool_s = [], [], []
    ret_s = None
    kp_l, vp_l, kip_l, ks_l, vs_l, kis_l = [], [], [], [], [], []
    for i in range(DEPTH):
        kind, j = i % 3, i // 3
        g1, b1 = ln_g[i, 0][None, :], ln_b[i, 0][None, :]
        g2, b2 = ln_g[i, 1][None, :], ln_b[i, 1][None, :]
        if kind == 0:
            w_in = ret_w_in[j].astype(BF16)
            w_out = ret_w_out[j].astype(BF16)
            ng = ret_norm_g[j][None, :]
            half = RET_KEY_DIM // 2
            cos_p, sin_p = _rope_tables_split(pos_p, half)
            cos_s, sin_s = _rope_tables_split(pos_s, half)
            q, k, v, g = _ret_proj(xp, w_in, cos_p, sin_p, lp // tmp, BF16)
            shp = lambda a: a.reshape(bp, lp, a.shape[-1])
            o, sp = _ret_core_prompt(shp(q), shp(k), shp(v), shp(g), ng)
            xp = _out_ln(o.reshape(mp, -1), w_out, xp, g1, b1)
            q, k, v, g = _ret_proj(xs, w_in, cos_s, sin_s, 1, F32)
            o, ret_s = _ret_core_sample(q[:, None, :], k[:, None, :], v[:, None, :], g[:, None, :],
                                        state_ret, j, ng, ret_s)
            xs = _out_ln(o.reshape(db, -1), w_out, xs, g1, b1)
            ret_p.append(sp)
        elif kind == 1:
            w_grp = pool_w_grp[j].astype(BF16)
            sc = pool_scale[j][None, :]
            pool_p.append(xp.reshape(bp, lp, d)[:, -POOL_HIST:])
            pool_s.append(jnp.concatenate([state_pool[j][:, 1:], xs[:, None, :]], axis=1))
            xp = _pool_layer_prompt(xp.reshape(bp, lp, d), w_grp, sc, g1, b1).reshape(mp, d)
            xs = _pool_layer_sample(xs, jnp.swapaxes(state_pool[j], 0, 1), w_grp, sc, g1, b1)
        else:
            w_pad = _dsa_pad_w_in(dsa_w_in[j]).astype(BF16)
            w_out = dsa_w_out[j].astype(BF16)
            tabs_p = _rope_tables_rolled(pos_p, ATTN_HEAD_DIM // 2) + _rope_tables_rolled(pos_p, IDX_DIM // 2)
            tabs_s = _rope_tables_rolled(pos_s, ATTN_HEAD_DIM // 2) + _rope_tables_rolled(pos_s, IDX_DIM // 2)
            qb, k32, v32, kb, vb, qi, kiw = _dsa_proj(xp, w_pad, tabs_p, lp // tmp)
            shp = lambda a: a.reshape(bp, lp, a.shape[-1])
            o = _dsa_attn_prompt(shp(qb), shp(qi), shp(kiw), shp(kb), shp(vb), min(TOPK_MAX, lp // 4))
            xp = _out_ln(o.reshape(mp, d), w_out, xp, g1, b1)
            kp_l.append(k32.reshape(bp, lp, ATTN_HEADS, ATTN_HEAD_DIM))
            vp_l.append(v32.reshape(bp, lp, ATTN_HEADS, ATTN_HEAD_DIM))
            kip_l.append(kiw[:, :IDX_DIM].reshape(bp, lp, IDX_DIM))

            qb, k32, v32, kb, vb, qi, kiw = _dsa_proj(xs, w_pad, tabs_s, 1)
            score_past = _dsa_idx_sample(page_table, qi, kiw, cache_kidx[j])
            bias = _dsa_sel_sample(score_past, qi, kiw, min(TOPK_MAX, (past + ls) // 4))
            heads = lambda a: a.reshape(db, ATTN_HEADS, ATTN_HEAD_DIM)
            o = _dsa_attn_sample(page_table, heads(qb), heads(k32), heads(v32), bias, cache_k[j], cache_v[j])
            xs = _out_ln(o.reshape(db, d), w_out, xs, g1, b1)
            ks_l.append(k32.reshape(db, ls, ATTN_HEADS, ATTN_HEAD_DIM))
            vs_l.append(v32.reshape(db, ls, ATTN_HEADS, ATTN_HEAD_DIM))
            kis_l.append(kiw[:, :IDX_DIM].reshape(db, ls, IDX_DIM))
        w1 = ffn_w_in[i].astype(BF16)
        w2 = ffn_w_out[i].astype(BF16)
        xp = _ffn_ln(xp, w1, w2, g2, b2)
        xs = _ffn_ln(xs, w1, w2, g2, b2)
    return (xp.reshape(bp, lp, d), xs.reshape(db, ls, d), jnp.stack(ret_p), ret_s,
            jnp.stack(pool_p), jnp.stack(pool_s), jnp.stack(kp_l), jnp.stack(vp_l), jnp.stack(kip_l),
            jnp.stack(ks_l), jnp.stack(vs_l), jnp.stack(kis_l))
```

```python
import functools
import math

import jax
import jax.numpy as jnp
from jax import lax
from jax.experimental import pallas as pl
from jax.experimental.pallas import tpu as pltpu

F32 = jnp.float32
BF16 = jnp.bfloat16
I32 = jnp.int32
I16 = jnp.int16

LANES = 128
VMEM_LIMIT_BYTES = 56 * 1024 * 1024

D_MODEL = 1024
DEPTH = 4
PAGE_SIZE = 128
RET_HEADS = 4
RET_KEY_DIM = D_MODEL // RET_HEADS
RET_VAL_DIM = 2 * RET_KEY_DIM
RET_BLOCK = 256
POOL_WINDOWS = (2, 4, 8, 16)
POOL_GROUP_DIM = D_MODEL // len(POOL_WINDOWS)
POOL_HIST = max(POOL_WINDOWS) - 1
ATTN_HEADS = 8
ATTN_HEAD_DIM = D_MODEL // ATTN_HEADS
IDX_HEADS = 4
IDX_DIM = 64
IDX_SCALE = (IDX_HEADS ** -0.5) * (IDX_DIM ** -0.5)
TOPK_MAX = 256
FFN_DIM = -(-8 * D_MODEL // (3 * 256)) * 256
ROPE_THETA = 10000.0
LN_EPS = 1e-5
DEEPNORM_ALPHA = (2.0 * DEPTH) ** 0.25
INT32_MIN = -(2 ** 31)
NEG_INF = float("-inf")


def _params(*sem):
    return pltpu.CompilerParams(dimension_semantics=sem, vmem_limit_bytes=VMEM_LIMIT_BYTES)


def _resident(shape):
    nd = len(shape)
    return pl.BlockSpec(shape, lambda *_: (0,) * nd, pipeline_mode=pl.Buffered(1))


def _row_tile(m):
    for t in (512, 256, 128):
        if m % t == 0:
            return t
    raise ValueError(f"token count {m} is not a multiple of 128")


def _layer_norm(z, g, b):
    mu = jnp.mean(z, axis=-1, keepdims=True)
    zc = z - mu
    var = jnp.mean(zc * zc, axis=-1, keepdims=True)
    return zc * lax.rsqrt(var + LN_EPS) * g + b


def _dot(a, b):
    return jnp.dot(a, b, preferred_element_type=F32)


def _dot_nt(a, b, precision=None):
    return lax.dot_general(a, b, (((1,), (1,)), ((), ())), precision=precision,
                           preferred_element_type=F32)


def _rope_angles(pos, half):
    inv = jnp.power(ROPE_THETA, -jnp.arange(half, dtype=F32) / half)
    return pos.astype(F32)[:, None] * inv[None, :]


def _rope_tables_split(pos, half):
    ang = _rope_angles(pos, half)
    return jnp.cos(ang), jnp.sin(ang)


def _rope_tables_rolled(pos, half, width=LANES):
    ang = _rope_angles(pos, half)
    cos = jnp.concatenate([jnp.cos(ang), jnp.cos(ang)], axis=-1)
    sin = jnp.concatenate([-jnp.sin(ang), jnp.sin(ang)], axis=-1)
    pad = width - 2 * half
    if pad:
        cos = jnp.concatenate([cos, jnp.ones((pos.shape[0], pad), F32)], axis=-1)
        sin = jnp.concatenate([sin, jnp.zeros((pos.shape[0], pad), F32)], axis=-1)
    return cos, sin


def _ret_proj_kernel(x_ref, w_ref, cos_ref, sin_ref, q_ref, k_ref, v_ref, g_ref):
    xb = x_ref[...].astype(BF16)
    cos = cos_ref[...]
    sin = sin_ref[...]
    hk = RET_HEADS * RET_KEY_DIM
    hv = RET_HEADS * RET_VAL_DIM
    half = RET_KEY_DIM // 2
    for h in range(RET_HEADS):
        for base, o_ref, scale in ((0, q_ref, None), (hk, k_ref, RET_KEY_DIM ** -0.5)):
            c0 = base + h * RET_KEY_DIM
            y = _dot(xb, w_ref[:, c0:c0 + RET_KEY_DIM])
            x1 = y[:, :half]
            x2 = y[:, half:]
            o1 = x1 * cos - x2 * sin
            o2 = x2 * cos + x1 * sin
            if scale is not None:
                o1 = o1 * scale
                o2 = o2 * scale
            o_ref[:, h * RET_KEY_DIM:h * RET_KEY_DIM + half] = o1.astype(o_ref.dtype)
            o_ref[:, h * RET_KEY_DIM + half:(h + 1) * RET_KEY_DIM] = o2.astype(o_ref.dtype)
    for h in range(RET_HEADS):
        c0 = h * RET_VAL_DIM
        v_ref[:, c0:c0 + RET_VAL_DIM] = _dot(
            xb, w_ref[:, 2 * hk + c0:2 * hk + c0 + RET_VAL_DIM]).astype(v_ref.dtype)
        g_ref[:, c0:c0 + RET_VAL_DIM] = _dot(
            xb, w_ref[:, 2 * hk + hv + c0:2 * hk + hv + c0 + RET_VAL_DIM]).astype(g_ref.dtype)


def _ret_proj(x, w, cos, sin, pos_tiles, out_dtype):
    m = x.shape[0]
    tm = _row_tile(m)
    hk = RET_HEADS * RET_KEY_DIM
    hv = RET_HEADS * RET_VAL_DIM
    half = RET_KEY_DIM // 2
    tab = pl.BlockSpec((tm, half), lambda i: (i % pos_tiles, 0))
    return pl.pallas_call(
        _ret_proj_kernel,
        grid=(m // tm,),
        in_specs=[pl.BlockSpec((tm, D_MODEL), lambda i: (i, 0)), _resident(w.shape), tab, tab],
        out_specs=[pl.BlockSpec((tm, hk), lambda i: (i, 0)), pl.BlockSpec((tm, hk), lambda i: (i, 0)),
                   pl.BlockSpec((tm, hv), lambda i: (i, 0)), pl.BlockSpec((tm, hv), lambda i: (i, 0))],
        out_shape=[jax.ShapeDtypeStruct((m, hk), out_dtype), jax.ShapeDtypeStruct((m, hk), out_dtype),
                   jax.ShapeDtypeStruct((m, hv), out_dtype), jax.ShapeDtypeStruct((m, hv), out_dtype)],
        compiler_params=_params("parallel"),
        name="ret_proj",
    )(x, w, cos, sin)


def _out_ln_kernel(a_ref, w_ref, x_ref, g_ref, b_ref, o_ref):
    y = _dot(a_ref[...].astype(BF16), w_ref[...])
    o_ref[...] = _layer_norm(DEEPNORM_ALPHA * x_ref[...] + y, g_ref[...], b_ref[...])


def _out_ln(a, w, x, g, b):
    m, ka = a.shape
    tm = _row_tile(m)
    return pl.pallas_call(
        _out_ln_kernel,
        grid=(m // tm,),
        in_specs=[pl.BlockSpec((tm, ka), lambda i: (i, 0)), _resident(w.shape),
                  pl.BlockSpec((tm, D_MODEL), lambda i: (i, 0)),
                  _resident((1, D_MODEL)), _resident((1, D_MODEL))],
        out_specs=pl.BlockSpec((tm, D_MODEL), lambda i: (i, 0)),
        out_shape=jax.ShapeDtypeStruct((m, D_MODEL), F32),
        compiler_params=_params("parallel"),
        name="out_ln",
    )(a, w, x, g, b)


FFN_CHUNK = 256


def _ffn_kernel(x_ref, w_in_ref, w_out_ref, g_ref, b_ref, o_ref, acc_ref):
    x = x_ref[...]
    xb = x.astype(BF16)
    for n, c in enumerate(range(0, FFN_DIM, FFN_CHUNK)):
        a = _dot(xb, w_in_ref[:, c:c + FFN_CHUNK])
        u = _dot(xb, w_in_ref[:, FFN_DIM + c:FFN_DIM + c + FFN_CHUNK])
        hact = (a * jax.nn.sigmoid(a) * u).astype(BF16)
        y = _dot(hact, w_out_ref[c:c + FFN_CHUNK, :])
        if n == 0:
            acc_ref[...] = y
        else:
            acc_ref[...] += y
    o_ref[...] = _layer_norm(DEEPNORM_ALPHA * x + acc_ref[...], g_ref[...], b_ref[...])


def _ffn_ln(x, w_in, w_out, g, b):
    m = x.shape[0]
    tm = _row_tile(m)
    return pl.pallas_call(
        _ffn_kernel,
        grid=(m // tm,),
        in_specs=[pl.BlockSpec((tm, D_MODEL), lambda i: (i, 0)), _resident(w_in.shape),
                  _resident(w_out.shape), _resident((1, D_MODEL)), _resident((1, D_MODEL))],
        out_specs=pl.BlockSpec((tm, D_MODEL), lambda i: (i, 0)),
        out_shape=jax.ShapeDtypeStruct((m, D_MODEL), F32),
        scratch_shapes=[pltpu.VMEM((tm, D_MODEL), F32)],
        compiler_params=_params("parallel"),
        name="ffn_ln",
    )(x, w_in, w_out, g, b)


def _group_norm_gate(o, norm_g, gate):
    mu = jnp.mean(o, axis=-1, keepdims=True)
    oc = o - mu
    var = jnp.mean(oc * oc, axis=-1, keepdims=True)
    on = oc * lax.rsqrt(var + LN_EPS) * norm_g
    return gate * jax.nn.sigmoid(gate) * on


def _ret_decay_tables(c):
    h = RET_HEADS
    lg = jnp.log1p(-jnp.exp2(-5.0 - jnp.arange(h, dtype=F32)))
    i = jnp.arange(c, dtype=F32)
    diff = i[:, None] - i[None, :]
    decay_in = jnp.where(diff[None] >= 0, jnp.exp(jnp.maximum(diff, 0.0)[None] * lg[:, None, None]), 0.0)
    q_dec = jnp.exp((i[None, :] + 1.0) * lg[:, None])
    k_dec = jnp.exp((c - 1.0 - i[None, :]) * lg[:, None])
    c_dec = jnp.exp(c * lg)
    dq = jnp.broadcast_to(q_dec[:, :, None], (h, c, RET_VAL_DIM))
    dk = jnp.broadcast_to(k_dec[:, :, None], (h, c, RET_VAL_DIM))
    dc = jnp.broadcast_to(c_dec[:, None, None], (h, 1, RET_VAL_DIM))
    return decay_in, dq, dk, dc


def _ret_core_kernel(q_ref, k_ref, v_ref, g_ref, din_ref, dq_ref, dk_ref, dc_ref, ng_ref,
                     o_ref, s_ref):
    c = pl.program_id(1)

    @pl.when(c == 0)
    def _():
        s_ref[...] = jnp.zeros_like(s_ref)

    for h in range(RET_HEADS):
        ks = slice(h * RET_KEY_DIM, (h + 1) * RET_KEY_DIM)
        vs = slice(h * RET_VAL_DIM, (h + 1) * RET_VAL_DIM)
        q = q_ref[0, :, ks]
        k = k_ref[0, :, ks]
        v = v_ref[0, :, vs]
        s = s_ref[0, h]
        att = _dot_nt(q, k) * din_ref[h]
        inner = _dot(att.astype(BF16), v)
        cross = _dot(q, s.astype(BF16)) * dq_ref[h]
        vd = (v.astype(F32) * dk_ref[h]).astype(BF16)
        kt = k.astype(F32).T.astype(BF16)
        s_ref[0, h] = s * dc_ref[h] + _dot(kt, vd)
        o_ref[0, :, vs] = _group_norm_gate(inner + cross, ng_ref[:, vs],
                                           g_ref[0, :, vs].astype(F32)).astype(o_ref.dtype)


def _ret_core_prompt(q, k, v, g, norm_g):
    b, l, _ = q.shape
    c = math.gcd(l, RET_BLOCK)
    hk = RET_HEADS * RET_KEY_DIM
    hv = RET_HEADS * RET_VAL_DIM
    decay_in, dq, dk, dc = _ret_decay_tables(c)
    kspec = pl.BlockSpec((1, c, hk), lambda bi, ci: (bi, ci, 0))
    vspec = pl.BlockSpec((1, c, hv), lambda bi, ci: (bi, ci, 0))
    return pl.pallas_call(
        _ret_core_kernel,
        grid=(b, l // c),
        in_specs=[kspec, kspec, vspec, vspec, _resident(decay_in.shape), _resident(dq.shape),
                  _resident(dk.shape), _resident(dc.shape), _resident(norm_g.shape)],
        out_specs=[vspec,
                   pl.BlockSpec((1, RET_HEADS, RET_KEY_DIM, RET_VAL_DIM), lambda bi, ci: (bi, 0, 0, 0))],
        out_shape=[jax.ShapeDtypeStruct((b, l, hv), BF16),
                   jax.ShapeDtypeStruct((b, RET_HEADS, RET_KEY_DIM, RET_VAL_DIM), F32)],
        compiler_params=_params("parallel", "arbitrary"),
        name="ret_core",
    )(q, k, v, g, decay_in, dq, dk, dc, norm_g)


def _column(row, n):
    r = lax.broadcasted_iota(I32, (n, n), 0)
    c = lax.broadcasted_iota(I32, (n, n), 1)
    return jnp.sum(jnp.where(r == c, jnp.broadcast_to(row, (n, n)), 0.0), axis=1, keepdims=True)


def _ret_sample_kernel(q_ref, k_ref, v_ref, g_ref, s_ref, gam_ref, ng_ref, *rest):
    o_ref, so_ref = rest[-2:]
    for r in range(q_ref.shape[0]):
        for h in range(RET_HEADS):
            ks = slice(h * RET_KEY_DIM, (h + 1) * RET_KEY_DIM)
            vs = slice(h * RET_VAL_DIM, (h + 1) * RET_VAL_DIM)
            qcol = _column(q_ref[r, :, ks], RET_KEY_DIM)
            kcol = _column(k_ref[r, :, ks], RET_KEY_DIM)
            v = v_ref[r, :, vs]
            s_new = s_ref[0, r, h] * gam_ref[h] + kcol * v
            so_ref[0, r, h] = s_new
            o = jnp.sum(qcol * s_new, axis=0, keepdims=True)
            o_ref[r, :, vs] = _group_norm_gate(o, ng_ref[:, vs], g_ref[r, :, vs])


def _ret_core_sample(q, k, v, g, state_ret, j, norm_g, new_states):
    db = q.shape[0]
    lg = jnp.log1p(-jnp.exp2(-5.0 - jnp.arange(RET_HEADS, dtype=F32)))
    gam = jnp.broadcast_to(jnp.exp(lg)[:, None, None], (RET_HEADS, 1, RET_VAL_DIM))
    tb = 2 if db % 2 == 0 else 1
    kspec = pl.BlockSpec((tb, 1, RET_HEADS * RET_KEY_DIM), lambda bi: (bi, 0, 0))
    vspec = pl.BlockSpec((tb, 1, RET_HEADS * RET_VAL_DIM), lambda bi: (bi, 0, 0))
    sspec = pl.BlockSpec((1, tb, RET_HEADS, RET_KEY_DIM, RET_VAL_DIM), lambda bi: (j, bi, 0, 0, 0))
    in_specs = [kspec, kspec, vspec, vspec, sspec, _resident(gam.shape), _resident(norm_g.shape)]
    args = [q, k, v, g, state_ret, gam, norm_g]
    aliases = {}
    if new_states is not None:
        in_specs.append(pl.BlockSpec(memory_space=pl.ANY))
        args.append(new_states)
        aliases = {len(args) - 1: 1}
    return pl.pallas_call(
        _ret_sample_kernel,
        grid=(db // tb,),
        in_specs=in_specs,
        out_specs=[vspec, sspec],
        out_shape=[jax.ShapeDtypeStruct((db, 1, RET_HEADS * RET_VAL_DIM), F32),
                   jax.ShapeDtypeStruct(state_ret.shape, F32)],
        input_output_aliases=aliases,
        compiler_params=_params("parallel"),
        name="ret_sample",
    )(*args)


POOL_HALO = 16


def _pool_prompt_kernel(x_ref, xh_ref, w_ref, sc_ref, g_ref, b_ref, o_ref, ext_ref, z_ref):
    i = pl.program_id(1)
    tm = x_ref.shape[1]
    x = x_ref[0]
    halo = jnp.where(i == 0, 0.0, xh_ref[0])
    ext_ref[0:POOL_HALO, :] = halo
    ext_ref[POOL_HALO:POOL_HALO + tm, :] = x
    pos = i * tm + lax.broadcasted_iota(I32, (tm, 1), 0)
    for gi, w in enumerate(POOL_WINDOWS):
        cs = slice(gi * POOL_GROUP_DIM, (gi + 1) * POOL_GROUP_DIM)
        wsum = ext_ref[POOL_HALO:POOL_HALO + tm, cs]
        for jj in range(1, w):
            wsum = wsum + ext_ref[POOL_HALO - jj:POOL_HALO - jj + tm, cs]
        cnt = jnp.minimum(w, pos + 1).astype(F32)
        d = wsum / cnt - x[:, cs]
        y = _dot(d.astype(BF16), w_ref[gi]) * sc_ref[:, cs]
        z_ref[:, cs] = DEEPNORM_ALPHA * x[:, cs] + y
    o_ref[0] = _layer_norm(z_ref[...], g_ref[...], b_ref[...])


def _pool_layer_prompt(x, w_grp, scale, g, b):
    bsz, l, _ = x.shape
    tm = _row_tile(l)
    hb = tm // POOL_HALO
    return pl.pallas_call(
        _pool_prompt_kernel,
        grid=(bsz, l // tm),
        in_specs=[pl.BlockSpec((1, tm, D_MODEL), lambda bi, i: (bi, i, 0)),
                  pl.BlockSpec((1, POOL_HALO, D_MODEL), lambda bi, i: (bi, jnp.maximum(i * hb - 1, 0), 0)),
                  _resident(w_grp.shape), _resident((1, D_MODEL)), _resident((1, D_MODEL)),
                  _resident((1, D_MODEL))],
        out_specs=pl.BlockSpec((1, tm, D_MODEL), lambda bi, i: (bi, i, 0)),
        out_shape=jax.ShapeDtypeStruct((bsz, l, D_MODEL), F32),
        scratch_shapes=[pltpu.VMEM((POOL_HALO + tm, D_MODEL), F32), pltpu.VMEM((tm, D_MODEL), F32)],
        compiler_params=_params("parallel", "arbitrary"),
        name="pool_prompt",
    )(x, x, w_grp, scale, g, b)


def _pool_sample_kernel(x_ref, h_ref, w_ref, sc_ref, g_ref, b_ref, o_ref, z_ref):
    x = x_ref[...]
    for gi, w in enumerate(POOL_WINDOWS):
        cs = slice(gi * POOL_GROUP_DIM, (gi + 1) * POOL_GROUP_DIM)
        wsum = x[:, cs]
        for jj in range(1, w):
            wsum = wsum + h_ref[POOL_HIST - jj, :, cs]
        d = wsum / float(w) - x[:, cs]
        y = _dot(d.astype(BF16), w_ref[gi]) * sc_ref[:, cs]
        z_ref[:, cs] = DEEPNORM_ALPHA * x[:, cs] + y
    o_ref[...] = _layer_norm(z_ref[...], g_ref[...], b_ref[...])


def _pool_layer_sample(x, hist_t, w_grp, scale, g, b):
    db = x.shape[0]
    return pl.pallas_call(
        _pool_sample_kernel,
        grid=(1,),
        in_specs=[_resident(x.shape), _resident(hist_t.shape), _resident(w_grp.shape),
                  _resident((1, D_MODEL)), _resident((1, D_MODEL)), _resident((1, D_MODEL))],
        out_specs=pl.BlockSpec((db, D_MODEL), lambda i: (0, 0)),
        out_shape=jax.ShapeDtypeStruct((db, D_MODEL), F32),
        scratch_shapes=[pltpu.VMEM((db, D_MODEL), F32)],
        compiler_params=_params("arbitrary"),
        name="pool_sample",
    )(x, hist_t, w_grp, scale, g, b)


DSA_QKV = 3 * D_MODEL
DSA_QI_PAD = IDX_HEADS * LANES
DSA_PAD_IN = DSA_QKV + DSA_QI_PAD + LANES
WI_LANE = IDX_DIM


def _dsa_pad_w_in(w_in):
    d = w_in.shape[0]
    qi0 = DSA_QKV
    ki0 = qi0 + IDX_HEADS * IDX_DIM
    z = jnp.zeros((d, LANES - IDX_DIM), w_in.dtype)
    parts = [w_in[:, :DSA_QKV]]
    for h in range(IDX_HEADS):
        parts += [w_in[:, qi0 + h * IDX_DIM:qi0 + (h + 1) * IDX_DIM], z]
    parts += [w_in[:, ki0:], jnp.zeros((d, LANES - IDX_DIM - IDX_HEADS), w_in.dtype)]
    return jnp.concatenate(parts, axis=1)


def _rope_rolled(y, cos, sin, half):
    if 2 * half == LANES:
        rot = pltpu.roll(y, half, 1)
    else:
        lane = lax.broadcasted_iota(I32, y.shape, 1)
        rot = jnp.where(lane < half, pltpu.roll(y, LANES - half, 1), pltpu.roll(y, half, 1))
    return y * cos + rot * sin


def _dsa_proj_kernel(x_ref, w_ref, cq_ref, sq_ref, ci_ref, si_ref,
                     qb_ref, k_ref, v_ref, kb_ref, vb_ref, qi_ref, kiw_ref):
    xb = x_ref[...].astype(BF16)
    cq = cq_ref[...]
    sq = sq_ref[...]
    ci = ci_ref[...]
    si = si_ref[...]
    hd = ATTN_HEAD_DIM
    for h in range(ATTN_HEADS):
        cs = slice(h * hd, (h + 1) * hd)
        yq = _dot(xb, w_ref[:, h * hd:(h + 1) * hd])
        qb_ref[:, cs] = _rope_rolled(yq, cq, sq, hd // 2).astype(BF16)
        yk = _rope_rolled(_dot(xb, w_ref[:, D_MODEL + h * hd:D_MODEL + (h + 1) * hd]), cq, sq, hd // 2)
        k_ref[:, cs] = yk
        kb_ref[:, cs] = yk.astype(BF16)
        yv = _dot(xb, w_ref[:, 2 * D_MODEL + h * hd:2 * D_MODEL + (h + 1) * hd])
        v_ref[:, cs] = yv
        vb_ref[:, cs] = yv.astype(BF16)
    for h in range(IDX_HEADS):
        c0 = DSA_QKV + h * LANES
        qi_ref[:, h * LANES:(h + 1) * LANES] = _rope_rolled(
            _dot(xb, w_ref[:, c0:c0 + LANES]), ci, si, IDX_DIM // 2)
    c0 = DSA_QKV + DSA_QI_PAD
    kiw_ref[...] = _rope_rolled(_dot(xb, w_ref[:, c0:c0 + LANES]), ci, si, IDX_DIM // 2)


def _dsa_proj(x, w_pad, tabs, pos_tiles):
    m = x.shape[0]
    tm = _row_tile(m)
    tab = pl.BlockSpec((tm, LANES), lambda i: (i % pos_tiles, 0))
    row = lambda n: pl.BlockSpec((tm, n), lambda i: (i, 0))
    sds = lambda n, dt: jax.ShapeDtypeStruct((m, n), dt)
    return pl.pallas_call(
        _dsa_proj_kernel,
        grid=(m // tm,),
        in_specs=[row(D_MODEL), _resident(w_pad.shape), tab, tab, tab, tab],
        out_specs=[row(D_MODEL), row(D_MODEL), row(D_MODEL), row(D_MODEL), row(D_MODEL),
                   row(DSA_QI_PAD), row(LANES)],
        out_shape=[sds(D_MODEL, BF16), sds(D_MODEL, F32), sds(D_MODEL, F32), sds(D_MODEL, BF16),
                   sds(D_MODEL, BF16), sds(DSA_QI_PAD, F32), sds(LANES, F32)],
        compiler_params=_params("parallel"),
        name="dsa_proj",
    )(x, w_pad, *tabs)


SEARCH_ROWS = 16


def _topk_bias_t(score, kidx, limit, n_top, idx_bits, hi_ref, lo_ref, eq_ref):
    n, q = score.shape
    slabs = n // SEARCH_ROWS
    one = jnp.ones((), I16)
    zero = jnp.zeros((), I16)
    low = jnp.int16(-32768)
    full = lambda v: jnp.broadcast_to(v.astype(I16), (n, q))
    slab = lambda v: jnp.broadcast_to(v.astype(I16), (SEARCH_ROWS, q))
    rows = lambda ref, s: ref[s * SEARCH_ROWS:(s + 1) * SEARCH_ROWS, :]
    bits = lax.bitcast_convert_type(score, I32)
    key = bits ^ ((bits >> 31) & 0x7FFFFFFF)
    vis = kidx.astype(I16) <= full(jnp.asarray(limit, I32))
    hi_ref[0:n, :] = jnp.where(vis, (key >> 16).astype(I16), low)
    lo = jnp.where(vis, key.astype(I16) ^ low, low)

    def count(pick):
        part = pick(0)
        for s in range(1, slabs):
            part = part + pick(s)
        return jnp.sum(part.astype(I32), axis=0, keepdims=True)

    def search16(ref, need):
        def count_ge(c):
            c16 = slab(c)
            return count(lambda s: jnp.where(rows(ref, s) >= c16, one, zero))

        z = jnp.zeros((1, q), I32)
        t = jnp.where(count_ge(z) >= need, z, -32768)

        def bit(i, t):
            cand = t + jnp.left_shift(jnp.int32(1), 14 - i)
            return jnp.where(count_ge(cand) >= need, cand, t)

        return lax.fori_loop(0, 15, bit, t)

    t_hi = search16(hi_ref, n_top)
    t_hi16 = slab(t_hi)
    need_lo = n_top - count(lambda s: jnp.where(rows(hi_ref, s) > t_hi16, one, zero))
    lo_ref[0:n, :] = jnp.where(hi_ref[0:n, :] == full(t_hi), lo, low)
    t_lo = search16(lo_ref, need_lo)
    t_lo16 = slab(t_lo)
    need = n_top - count(lambda s: jnp.where(rows(hi_ref, s) > t_hi16, one,
                                             jnp.where(rows(lo_ref, s) > t_lo16, one, zero)))
    eq_ref[0:n, :] = jnp.where(hi_ref[0:n, :] == full(t_hi),
                               jnp.where(lo_ref[0:n, :] == full(t_lo), one, zero), zero)
    sub = lax.broadcasted_iota(I16, (SEARCH_ROWS, q), 0)

    def index_bit(i, j):
        cand = j + jnp.left_shift(jnp.int32(1), idx_bits - 1 - i)
        c16 = slab(cand)
        before = count(lambda s: jnp.where(sub + jnp.int16(s * SEARCH_ROWS) < c16, rows(eq_ref, s), zero))
        return jnp.where(before < need, cand, j)

    j = lax.fori_loop(0, idx_bits, index_bit, jnp.zeros((1, q), I32))
    taken = jnp.where(hi_ref[0:n, :] > full(t_hi), one,
                      jnp.where(lo_ref[0:n, :] > full(t_lo), one,
                                jnp.where(kidx.astype(I16) <= full(j), eq_ref[0:n, :], zero)))
    taken = jnp.where(vis, taken, zero)
    return jnp.where(taken.astype(I32) > 0, 0.0, NEG_INF)


def _stack_idx_heads(qi):
    return jnp.concatenate([qi[:, h * LANES:h * LANES + IDX_DIM] for h in range(IDX_HEADS)],
                           axis=0).astype(BF16)


def _dsa_attn_prompt_kernel(q_ref, qi_ref, wt_ref, kiw_ref, kb_ref, vb_ref, o_ref,
                            hi_ref, lo_ref, eq_ref, *, n_top, widths):
    i = pl.program_id(1)
    tq = q_ref.shape[1]
    hd = ATTN_HEAD_DIM

    def attend(n):
        r = _dot_nt(kiw_ref[0, :n, :IDX_DIM].astype(BF16), _stack_idx_heads(qi_ref[0]))
        score = jnp.zeros((n, tq), F32)
        for h in range(IDX_HEADS):
            score = score + (wt_ref[0, h:h + 1, :] * IDX_SCALE) * jnp.maximum(r[:, h * tq:(h + 1) * tq], 0.0)
        kidx = lax.broadcasted_iota(I32, (n, tq), 0)
        pos = i * tq + lax.broadcasted_iota(I32, (1, tq), 1)
        bias = _topk_bias_t(score, kidx, pos, n_top, max(1, (n - 1).bit_length()),
                            hi_ref, lo_ref, eq_ref).T
        for h in range(ATTN_HEADS):
            cs = slice(h * hd, (h + 1) * hd)
            s = _dot_nt(q_ref[0, :, cs], kb_ref[0, :n, cs]) * (hd ** -0.5) + bias
            m = jnp.max(s, axis=-1, keepdims=True)
            p = jnp.exp(s - m)
            l = jnp.sum(p, axis=-1, keepdims=True)
            o_ref[0, :, cs] = (_dot(p.astype(BF16), vb_ref[0, :n, cs]) / l).astype(o_ref.dtype)

    last = (i + 1) * tq
    prev = 0
    for n in widths:
        pl.when(jnp.logical_and(last > prev, last <= n))(functools.partial(attend, n))
        prev = n


def _dsa_attn_prompt(qb, qi, kiw, kb, vb, n_top):
    b, l, _ = qb.shape
    tq = 256 if l % 256 == 0 else 128
    widths = tuple(range(tq, l + 1, tq))
    wt = jnp.swapaxes(kiw[:, :, WI_LANE:WI_LANE + IDX_HEADS], 1, 2)
    qrow = lambda n: pl.BlockSpec((1, tq, n), lambda bi, i: (bi, i, 0))
    full = lambda n: pl.BlockSpec((1, l, n), lambda bi, i: (bi, 0, 0))
    return pl.pallas_call(
        functools.partial(_dsa_attn_prompt_kernel, n_top=n_top, widths=widths),
        grid=(b, l // tq),
        in_specs=[qrow(D_MODEL), qrow(DSA_QI_PAD), pl.BlockSpec((1, IDX_HEADS, tq), lambda bi, i: (bi, 0, i)),
                  full(LANES), full(D_MODEL), full(D_MODEL)],
        out_specs=qrow(D_MODEL),
        out_shape=jax.ShapeDtypeStruct((b, l, D_MODEL), BF16),
        scratch_shapes=[pltpu.VMEM((l, tq), I16)] * 3,
        compiler_params=_params("parallel", "arbitrary"),
        name="dsa_attn_prompt",
    )(qb, qi, wt, kiw, kb, vb)


def _dsa_idx_sample_kernel(pt_ref, qi_ref, kiwq_ref, *rest):
    del pt_ref
    pages, o_ref = rest[:-1], rest[-1]
    qs = _stack_idx_heads(qi_ref[0])
    wq = kiwq_ref[0]
    rows = wq.shape[0]
    for p, page_ref in enumerate(pages):
        r = _dot(qs, page_ref[0].astype(BF16))
        acc = jnp.zeros((rows, PAGE_SIZE), F32)
        for h in range(IDX_HEADS):
            acc = acc + (wq[:, WI_LANE + h:WI_LANE + h + 1] * IDX_SCALE) * jnp.maximum(
                r[h * rows:(h + 1) * rows], 0.0)
        o_ref[0, :, p * PAGE_SIZE:(p + 1) * PAGE_SIZE] = acc


def _dsa_idx_sample(page_table, qi, kiw, cache_kidx_t):
    db, n_pages = page_table.shape
    qi8 = jnp.broadcast_to(qi[:, None, :], (db, 8, DSA_QI_PAD))
    kiw8 = jnp.broadcast_to(kiw[:, None, :], (db, 8, LANES))
    page_specs = [pl.BlockSpec((1, IDX_DIM, PAGE_SIZE), functools.partial(lambda p, bi, pt: (pt[bi, p], 0, 0), p))
                  for p in range(n_pages)]
    out = pl.pallas_call(
        _dsa_idx_sample_kernel,
        grid_spec=pltpu.PrefetchScalarGridSpec(
            num_scalar_prefetch=1,
            grid=(db,),
            in_specs=[pl.BlockSpec((1, 8, DSA_QI_PAD), lambda bi, pt: (bi, 0, 0)),
                      pl.BlockSpec((1, 8, LANES), lambda bi, pt: (bi, 0, 0))] + page_specs,
            out_specs=pl.BlockSpec((1, 8, n_pages * PAGE_SIZE), lambda bi, pt: (bi, 0, 0)),
        ),
        out_shape=jax.ShapeDtypeStruct((db, 8, n_pages * PAGE_SIZE), F32),
        compiler_params=_params("arbitrary"),
        name="dsa_idx_sample",
    )(page_table, qi8, kiw8, *([cache_kidx_t] * n_pages))
    return out[:, 0, :]


def _dsa_sel_sample_kernel(spt_ref, qi_ref, kiw_ref, o_ref, hi_ref, lo_ref, eq_ref, *, past, n_top, idx_bits):
    db = qi_ref.shape[0]
    kiw = kiw_ref[...]
    lane = lax.broadcasted_iota(I32, kiw.shape, 1)
    ki = jnp.where(lane < IDX_DIM, kiw, 0.0)
    qi = qi_ref[...]
    self_score = jnp.zeros((db, 1), F32)
    for h in range(IDX_HEADS):
        r = jnp.sum(qi[:, h * LANES:(h + 1) * LANES] * ki, axis=-1, keepdims=True)
        self_score = self_score + (kiw[:, WI_LANE + h:WI_LANE + h + 1] * IDX_SCALE) * jnp.maximum(r, 0.0)
    self_rows = jnp.broadcast_to(self_score, (db, LANES)).T[0:SEARCH_ROWS, :]
    n = past + SEARCH_ROWS
    score = jnp.concatenate([spt_ref[...], self_rows], axis=0)
    kidx = lax.broadcasted_iota(I32, (n, db), 0)
    o_ref[...] = _topk_bias_t(score, kidx, past, n_top, idx_bits, hi_ref, lo_ref, eq_ref)


def _dsa_sel_sample(score_past, qi, kiw, n_top):
    db, past = score_past.shape
    n = past + SEARCH_ROWS
    spt = score_past.T
    out = pl.pallas_call(
        functools.partial(_dsa_sel_sample_kernel, past=past, n_top=n_top, idx_bits=n.bit_length()),
        grid=(1,),
        in_specs=[_resident(spt.shape), _resident(qi.shape), _resident(kiw.shape)],
        out_specs=pl.BlockSpec((n, db), lambda i: (0, 0)),
        out_shape=jax.ShapeDtypeStruct((n, db), F32),
        scratch_shapes=[pltpu.VMEM((n, db), I16)] * 3,
        compiler_params=_params("arbitrary"),
        name="dsa_sel_sample",
    )(spt, qi, kiw)
    return out.T


def _dsa_attn_sample_kernel(pt_ref, q_ref, kn_ref, vn_ref, bias_ref, bself_ref, *rest, n_pages):
    del pt_ref
    kpages, vpages, o_ref = rest[:n_pages], rest[n_pages:2 * n_pages], rest[-1]
    hd = ATTN_HEAD_DIM
    rows = PAGE_SIZE * ATTN_HEADS
    qb = q_ref[0]
    scale = hd ** -0.5
    head = lax.broadcasted_iota(I32, (ATTN_HEADS, rows), 0)
    col = lax.broadcasted_iota(I32, (ATTN_HEADS, rows), 1)
    own = jnp.bitwise_and(col, ATTN_HEADS - 1) == head
    s_pages = []
    for p, kp in enumerate(kpages):
        k2 = kp[0].reshape(rows, hd).astype(BF16)
        s = _dot_nt(qb, k2) * scale + bias_ref[0, :, p * rows:(p + 1) * rows]
        s_pages.append(jnp.where(own, s, NEG_INF))
    s_self = jnp.sum(qb.astype(F32) * kn_ref[0].astype(BF16).astype(F32), axis=-1, keepdims=True) * scale
    s_self = s_self + bself_ref[0, :, 0:1]
    m = s_self
    for s in s_pages:
        m = jnp.maximum(m, jnp.max(s, axis=-1, keepdims=True))
    p_self = jnp.exp(s_self - m)
    l = p_self
    acc = p_self.astype(BF16).astype(F32) * vn_ref[0].astype(BF16).astype(F32)
    for s, vp in zip(s_pages, vpages):
        pexp = jnp.exp(s - m)
        l = l + jnp.sum(pexp, axis=-1, keepdims=True)
        acc = acc + _dot(pexp.astype(BF16), vp[0].reshape(rows, hd).astype(BF16))
    o_ref[0] = acc / l


def _dsa_attn_sample(page_table, q, k_new, v_new, bias, cache_k, cache_v):
    db, n_pages = page_table.shape
    past = n_pages * PAGE_SIZE
    bias_keys = jnp.repeat(bias[:, :past], ATTN_HEADS, axis=1)[:, None, :]
    bias_self = bias[:, None, past:]
    row = pl.BlockSpec((1, ATTN_HEADS, ATTN_HEAD_DIM), lambda bi, pt: (bi, 0, 0))
    page_spec = lambda p: pl.BlockSpec((1, PAGE_SIZE, ATTN_HEADS, ATTN_HEAD_DIM),
                                       functools.partial(lambda p, bi, pt: (pt[bi, p], 0, 0, 0), p))
    return pl.pallas_call(
        functools.partial(_dsa_attn_sample_kernel, n_pages=n_pages),
        grid_spec=pltpu.PrefetchScalarGridSpec(
            num_scalar_prefetch=1,
            grid=(db,),
            in_specs=[row, row, row,
                      pl.BlockSpec((1, 1, past * ATTN_HEADS), lambda bi, pt: (bi, 0, 0)),
                      pl.BlockSpec((1, 1, bias_self.shape[-1]), lambda bi, pt: (bi, 0, 0))]
                     + [page_spec(p) for p in range(n_pages)] + [page_spec(p) for p in range(n_pages)],
            out_specs=row,
        ),
        out_shape=jax.ShapeDtypeStruct((db, ATTN_HEADS, ATTN_HEAD_DIM), F32),
        compiler_params=_params("arbitrary"),
        name="dsa_attn_sample",
    )(page_table, q, k_new, v_new, bias_keys, bias_self, *([cache_k] * n_pages), *([cache_v] * n_pages))


def kernel(x_prompt, x_sample, state_ret, state_pool, cache_k, cache_v, cache_kidx, page_table,
           ret_w_in, ret_norm_g, ret_w_out, pool_w_grp, pool_scale, dsa_w_in, dsa_w_out,
           ffn_w_in, ffn_w_out, ln_g, ln_b):
    bp, lp, d = x_prompt.shape
    db, ls, _ = x_sample.shape
    assert d == D_MODEL and ls == 1
    past = page_table.shape[1] * PAGE_SIZE
    mp = bp * lp
    tmp = _row_tile(mp)
    assert lp % tmp == 0
    xp = x_prompt.reshape(mp, d)
    xs = x_sample.reshape(db, d)
    pos_p = jnp.arange(lp)
    pos_s = jnp.full((db,), past, jnp.int32)

    ret_p, pool_p, pool_s = [], [], []
    ret_s = None
    kp_l, vp_l, kip_l, ks_l, vs_l, kis_l = [], [], [], [], [], []
    for i in range(DEPTH):
        kind, j = i % 3, i // 3
        g1, b1 = ln_g[i, 0][None, :], ln_b[i, 0][None, :]
        g2, b2 = ln_g[i, 1][None, :], ln_b[i, 1][None, :]
        if kind == 0:
            w_in = ret_w_in[j].astype(BF16)
            w_out = ret_w_out[j].astype(BF16)
            ng = ret_norm_g[j][None, :]
            half = RET_KEY_DIM // 2
            cos_p, sin_p = _rope_tables_split(pos_p, half)
            cos_s, sin_s = _rope_tables_split(pos_s, half)
            q, k, v, g = _ret_proj(xp, w_in, cos_p, sin_p, lp // tmp, BF16)
            shp = lambda a: a.reshape(bp, lp, a.shape[-1])
            o, sp = _ret_core_prompt(shp(q), shp(k), shp(v), shp(g), ng)
            xp = _out_ln(o.reshape(mp, -1), w_out, xp, g1, b1)
            q, k, v, g = _ret_proj(xs, w_in, cos_s, sin_s, 1, F32)
            o, ret_s = _ret_core_sample(q[:, None, :], k[:, None, :], v[:, None, :], g[:, None, :],
                                        state_ret, j, ng, ret_s)
            xs = _out_ln(o.reshape(db, -1), w_out, xs, g1, b1)
            ret_p.append(sp)
        elif kind == 1:
            w_grp = pool_w_grp[j].astype(BF16)
            sc = pool_scale[j][None, :]
            pool_p.append(xp.reshape(bp, lp, d)[:, -POOL_HIST:])
            pool_s.append(jnp.concatenate([state_pool[j][:, 1:], xs[:, None, :]], axis=1))
            xp = _pool_layer_prompt(xp.reshape(bp, lp, d), w_grp, sc, g1, b1).reshape(mp, d)
            xs = _pool_layer_sample(xs, jnp.swapaxes(state_pool[j], 0, 1), w_grp, sc, g1, b1)
        else:
            w_pad = _dsa_pad_w_in(dsa_w_in[j]).astype(BF16)
            w_out = dsa_w_out[j].astype(BF16)
            tabs_p = _rope_tables_rolled(pos_p, ATTN_HEAD_DIM // 2) + _rope_tables_rolled(pos_p, IDX_DIM // 2)
            tabs_s = _rope_tables_rolled(pos_s, ATTN_HEAD_DIM // 2) + _rope_tables_rolled(pos_s, IDX_DIM // 2)
            qb, k32, v32, kb, vb, qi, kiw = _dsa_proj(xp, w_pad, tabs_p, lp // tmp)
            shp = lambda a: a.reshape(bp, lp, a.shape[-1])
            o = _dsa_attn_prompt(shp(qb), shp(qi), shp(kiw), shp(kb), shp(vb), min(TOPK_MAX, lp // 4))
            xp = _out_ln(o.reshape(mp, d), w_out, xp, g1, b1)
            kp_l.append(k32.reshape(bp, lp, ATTN_HEADS, ATTN_HEAD_DIM))
            vp_l.append(v32.reshape(bp, lp, ATTN_HEADS, ATTN_HEAD_DIM))
            kip_l.append(kiw[:, :IDX_DIM].reshape(bp, lp, IDX_DIM))

            qb, k32, v32, kb, vb, qi, kiw = _dsa_proj(xs, w_pad, tabs_s, 1)
            score_past = _dsa_idx_sample(page_table, qi, kiw, jnp.swapaxes(cache_kidx[j], -1, -2))
            bias = _dsa_sel_sample(score_past, qi, kiw, min(TOPK_MAX, (past + ls) // 4))
            heads = lambda a: a.reshape(db, ATTN_HEADS, ATTN_HEAD_DIM)
            o = _dsa_attn_sample(page_table, heads(qb), heads(k32), heads(v32), bias, cache_k[j], cache_v[j])
            xs = _out_ln(o.reshape(db, d), w_out, xs, g1, b1)
            ks_l.append(k32.reshape(db, ls, ATTN_HEADS, ATTN_HEAD_DIM))
            vs_l.append(v32.reshape(db, ls, ATTN_HEADS, ATTN_HEAD_DIM))
            kis_l.append(kiw[:, :IDX_DIM].reshape(db, ls, IDX_DIM))
        w1 = ffn_w_in[i].astype(BF16)
        w2 = ffn_w_out[i].astype(BF16)
        xp = _ffn_ln(xp, w1, w2, g2, b2)
        xs = _ffn_ln(xs, w1, w2, g2, b2)
    return (xp.reshape(bp, lp, d), xs.reshape(db, ls, d), jnp.stack(ret_p), ret_s,
            jnp.stack(pool_p), jnp.stack(pool_s), jnp.stack(kp_l), jnp.stack(vp_l), jnp.stack(kip_l),
            jnp.stack(ks_l), jnp.stack(vs_l), jnp.stack(kis_l))
```

```python
import functools
import math

import jax
import jax.numpy as jnp
from jax import lax
from jax.experimental import pallas as pl
from jax.experimental.pallas import tpu as pltpu

F32 = jnp.float32
BF16 = jnp.bfloat16
I32 = jnp.int32
I16 = jnp.int16

LANES = 128
VMEM_LIMIT_BYTES = 56 * 1024 * 1024

D_MODEL = 1024
DEPTH = 4
PAGE_SIZE = 128
RET_HEADS = 4
RET_KEY_DIM = D_MODEL // RET_HEADS
RET_VAL_DIM = 2 * RET_KEY_DIM
RET_BLOCK = 256
POOL_WINDOWS = (2, 4, 8, 16)
POOL_GROUP_DIM = D_MODEL // len(POOL_WINDOWS)
POOL_HIST = max(POOL_WINDOWS) - 1
ATTN_HEADS = 8
ATTN_HEAD_DIM = D_MODEL // ATTN_HEADS
IDX_HEADS = 4
IDX_DIM = 64
IDX_SCALE = (IDX_HEADS ** -0.5) * (IDX_DIM ** -0.5)
TOPK_MAX = 256
FFN_DIM = -(-8 * D_MODEL // (3 * 256)) * 256
ROPE_THETA = 10000.0
LN_EPS = 1e-5
DEEPNORM_ALPHA = (2.0 * DEPTH) ** 0.25
INT32_MIN = -(2 ** 31)
NEG_INF = float("-inf")


def _params(*sem):
    return pltpu.CompilerParams(dimension_semantics=sem, vmem_limit_bytes=VMEM_LIMIT_BYTES)


def _resident(shape):
    nd = len(shape)
    return pl.BlockSpec(shape, lambda *_: (0,) * nd, pipeline_mode=pl.Buffered(1))


def _resident_layer(stacked, layer):
    shape = stacked.shape[1:]
    nd = len(shape)
    return pl.BlockSpec((None,) + shape, lambda *_: (layer,) + (0,) * nd, pipeline_mode=pl.Buffered(1))


def _row_tile(m):
    for t in (512, 256, 128):
        if m % t == 0:
            return t
    raise ValueError(f"token count {m} is not a multiple of 128")


def _layer_norm(z, g, b):
    mu = jnp.mean(z, axis=-1, keepdims=True)
    zc = z - mu
    var = jnp.mean(zc * zc, axis=-1, keepdims=True)
    return zc * lax.rsqrt(var + LN_EPS) * g + b


def _dot(a, b):
    return jnp.dot(a, b, preferred_element_type=F32)


def _dot_nt(a, b, precision=None):
    return lax.dot_general(a, b, (((1,), (1,)), ((), ())), precision=precision,
                           preferred_element_type=F32)


def _rope_angles(pos, half):
    inv = jnp.power(ROPE_THETA, -jnp.arange(half, dtype=F32) / half)
    return pos.astype(F32)[:, None] * inv[None, :]


def _rope_tables_split(pos, half):
    ang = _rope_angles(pos, half)
    return jnp.cos(ang), jnp.sin(ang)


def _rope_tables_rolled(pos, half, width=LANES):
    ang = _rope_angles(pos, half)
    cos = jnp.concatenate([jnp.cos(ang), jnp.cos(ang)], axis=-1)
    sin = jnp.concatenate([-jnp.sin(ang), jnp.sin(ang)], axis=-1)
    pad = width - 2 * half
    if pad:
        cos = jnp.concatenate([cos, jnp.ones((pos.shape[0], pad), F32)], axis=-1)
        sin = jnp.concatenate([sin, jnp.zeros((pos.shape[0], pad), F32)], axis=-1)
    return cos, sin


def _ret_proj_kernel(x_ref, w_ref, cos_ref, sin_ref, q_ref, k_ref, v_ref, g_ref):
    xb = x_ref[...].astype(BF16)
    cos = cos_ref[...]
    sin = sin_ref[...]
    hk = RET_HEADS * RET_KEY_DIM
    hv = RET_HEADS * RET_VAL_DIM
    half = RET_KEY_DIM // 2
    for h in range(RET_HEADS):
        for base, o_ref, scale in ((0, q_ref, None), (hk, k_ref, RET_KEY_DIM ** -0.5)):
            c0 = base + h * RET_KEY_DIM
            y = _dot(xb, w_ref[:, c0:c0 + RET_KEY_DIM])
            x1 = y[:, :half]
            x2 = y[:, half:]
            o1 = x1 * cos - x2 * sin
            o2 = x2 * cos + x1 * sin
            if scale is not None:
                o1 = o1 * scale
                o2 = o2 * scale
            o_ref[:, h * RET_KEY_DIM:h * RET_KEY_DIM + half] = o1.astype(o_ref.dtype)
            o_ref[:, h * RET_KEY_DIM + half:(h + 1) * RET_KEY_DIM] = o2.astype(o_ref.dtype)
    for h in range(RET_HEADS):
        c0 = h * RET_VAL_DIM
        v_ref[:, c0:c0 + RET_VAL_DIM] = _dot(
            xb, w_ref[:, 2 * hk + c0:2 * hk + c0 + RET_VAL_DIM]).astype(v_ref.dtype)
        g_ref[:, c0:c0 + RET_VAL_DIM] = _dot(
            xb, w_ref[:, 2 * hk + hv + c0:2 * hk + hv + c0 + RET_VAL_DIM]).astype(g_ref.dtype)


def _ret_proj(x, w, layer, cos, sin, pos_tiles, out_dtype):
    m = x.shape[0]
    tm = _row_tile(m)
    hk = RET_HEADS * RET_KEY_DIM
    hv = RET_HEADS * RET_VAL_DIM
    half = RET_KEY_DIM // 2
    tab = pl.BlockSpec((tm, half), lambda i: (i % pos_tiles, 0))
    return pl.pallas_call(
        _ret_proj_kernel,
        grid=(m // tm,),
        in_specs=[pl.BlockSpec((tm, D_MODEL), lambda i: (i, 0)), _resident_layer(w, layer), tab, tab],
        out_specs=[pl.BlockSpec((tm, hk), lambda i: (i, 0)), pl.BlockSpec((tm, hk), lambda i: (i, 0)),
                   pl.BlockSpec((tm, hv), lambda i: (i, 0)), pl.BlockSpec((tm, hv), lambda i: (i, 0))],
        out_shape=[jax.ShapeDtypeStruct((m, hk), out_dtype), jax.ShapeDtypeStruct((m, hk), out_dtype),
                   jax.ShapeDtypeStruct((m, hv), out_dtype), jax.ShapeDtypeStruct((m, hv), out_dtype)],
        compiler_params=_params("parallel"),
        name="ret_proj",
    )(x, w, cos, sin)


def _out_ln_kernel(a_ref, w_ref, x_ref, g_ref, b_ref, o_ref):
    y = _dot(a_ref[...].astype(BF16), w_ref[...])
    o_ref[...] = _layer_norm(DEEPNORM_ALPHA * x_ref[...] + y, g_ref[...], b_ref[...])


def _out_ln(a, w, layer, x, g, b):
    m, ka = a.shape
    tm = _row_tile(m)
    return pl.pallas_call(
        _out_ln_kernel,
        grid=(m // tm,),
        in_specs=[pl.BlockSpec((tm, ka), lambda i: (i, 0)), _resident_layer(w, layer),
                  pl.BlockSpec((tm, D_MODEL), lambda i: (i, 0)),
                  _resident((1, D_MODEL)), _resident((1, D_MODEL))],
        out_specs=pl.BlockSpec((tm, D_MODEL), lambda i: (i, 0)),
        out_shape=jax.ShapeDtypeStruct((m, D_MODEL), F32),
        compiler_params=_params("parallel"),
        name="out_ln",
    )(a, w, x, g, b)


FFN_CHUNK = 256


def _ffn_kernel(x_ref, w_in_ref, w_out_ref, g_ref, b_ref, o_ref, acc_ref):
    x = x_ref[...]
    xb = x.astype(BF16)
    for n, c in enumerate(range(0, FFN_DIM, FFN_CHUNK)):
        a = _dot(xb, w_in_ref[:, c:c + FFN_CHUNK])
        u = _dot(xb, w_in_ref[:, FFN_DIM + c:FFN_DIM + c + FFN_CHUNK])
        hact = (a * jax.nn.sigmoid(a) * u).astype(BF16)
        y = _dot(hact, w_out_ref[c:c + FFN_CHUNK, :])
        if n == 0:
            acc_ref[...] = y
        else:
            acc_ref[...] += y
    o_ref[...] = _layer_norm(DEEPNORM_ALPHA * x + acc_ref[...], g_ref[...], b_ref[...])


def _ffn_ln(x, w_in, w_out, layer, g, b):
    m = x.shape[0]
    tm = _row_tile(m)
    return pl.pallas_call(
        _ffn_kernel,
        grid=(m // tm,),
        in_specs=[pl.BlockSpec((tm, D_MODEL), lambda i: (i, 0)), _resident_layer(w_in, layer),
                  _resident_layer(w_out, layer), _resident((1, D_MODEL)), _resident((1, D_MODEL))],
        out_specs=pl.BlockSpec((tm, D_MODEL), lambda i: (i, 0)),
        out_shape=jax.ShapeDtypeStruct((m, D_MODEL), F32),
        scratch_shapes=[pltpu.VMEM((tm, D_MODEL), F32)],
        compiler_params=_params("parallel"),
        name="ffn_ln",
    )(x, w_in, w_out, g, b)


def _group_norm_gate(o, norm_g, gate):
    mu = jnp.mean(o, axis=-1, keepdims=True)
    oc = o - mu
    var = jnp.mean(oc * oc, axis=-1, keepdims=True)
    on = oc * lax.rsqrt(var + LN_EPS) * norm_g
    return gate * jax.nn.sigmoid(gate) * on


def _ret_decay_tables(c):
    h = RET_HEADS
    lg = jnp.log1p(-jnp.exp2(-5.0 - jnp.arange(h, dtype=F32)))
    i = jnp.arange(c, dtype=F32)
    diff = i[:, None] - i[None, :]
    decay_in = jnp.where(diff[None] >= 0, jnp.exp(jnp.maximum(diff, 0.0)[None] * lg[:, None, None]), 0.0)
    q_dec = jnp.exp((i[None, :] + 1.0) * lg[:, None])
    k_dec = jnp.exp((c - 1.0 - i[None, :]) * lg[:, None])
    c_dec = jnp.exp(c * lg)
    dq = jnp.broadcast_to(q_dec[:, :, None], (h, c, RET_VAL_DIM))
    dk = jnp.broadcast_to(k_dec[:, :, None], (h, c, RET_VAL_DIM))
    dc = jnp.broadcast_to(c_dec[:, None, None], (h, 1, RET_VAL_DIM))
    return decay_in, dq, dk, dc


def _ret_core_kernel(q_ref, k_ref, v_ref, g_ref, din_ref, dq_ref, dk_ref, dc_ref, ng_ref,
                     o_ref, s_ref):
    c = pl.program_id(1)

    @pl.when(c == 0)
    def _():
        s_ref[...] = jnp.zeros_like(s_ref)

    for h in range(RET_HEADS):
        ks = slice(h * RET_KEY_DIM, (h + 1) * RET_KEY_DIM)
        vs = slice(h * RET_VAL_DIM, (h + 1) * RET_VAL_DIM)
        q = q_ref[0, :, ks]
        k = k_ref[0, :, ks]
        v = v_ref[0, :, vs]
        s = s_ref[0, h]
        att = _dot_nt(q, k) * din_ref[h]
        inner = _dot(att.astype(BF16), v)
        cross = _dot(q, s.astype(BF16)) * dq_ref[h]
        vd = (v.astype(F32) * dk_ref[h]).astype(BF16)
        kt = k.astype(F32).T.astype(BF16)
        s_ref[0, h] = s * dc_ref[h] + _dot(kt, vd)
        o_ref[0, :, vs] = _group_norm_gate(inner + cross, ng_ref[:, vs],
                                           g_ref[0, :, vs].astype(F32)).astype(o_ref.dtype)


def _ret_core_prompt(q, k, v, g, norm_g):
    b, l, _ = q.shape
    c = math.gcd(l, RET_BLOCK)
    hk = RET_HEADS * RET_KEY_DIM
    hv = RET_HEADS * RET_VAL_DIM
    decay_in, dq, dk, dc = _ret_decay_tables(c)
    kspec = pl.BlockSpec((1, c, hk), lambda bi, ci: (bi, ci, 0))
    vspec = pl.BlockSpec((1, c, hv), lambda bi, ci: (bi, ci, 0))
    return pl.pallas_call(
        _ret_core_kernel,
        grid=(b, l // c),
        in_specs=[kspec, kspec, vspec, vspec, _resident(decay_in.shape), _resident(dq.shape),
                  _resident(dk.shape), _resident(dc.shape), _resident(norm_g.shape)],
        out_specs=[vspec,
                   pl.BlockSpec((1, RET_HEADS, RET_KEY_DIM, RET_VAL_DIM), lambda bi, ci: (bi, 0, 0, 0))],
        out_shape=[jax.ShapeDtypeStruct((b, l, hv), BF16),
                   jax.ShapeDtypeStruct((b, RET_HEADS, RET_KEY_DIM, RET_VAL_DIM), F32)],
        compiler_params=_params("parallel", "arbitrary"),
        name="ret_core",
    )(q, k, v, g, decay_in, dq, dk, dc, norm_g)


def _column(row, n):
    r = lax.broadcasted_iota(I32, (n, n), 0)
    c = lax.broadcasted_iota(I32, (n, n), 1)
    return jnp.sum(jnp.where(r == c, jnp.broadcast_to(row, (n, n)), 0.0), axis=1, keepdims=True)


def _ret_sample_kernel(q_ref, k_ref, v_ref, g_ref, s_ref, gam_ref, ng_ref, *rest):
    o_ref, so_ref = rest[-2:]
    for r in range(q_ref.shape[0]):
        for h in range(RET_HEADS):
            ks = slice(h * RET_KEY_DIM, (h + 1) * RET_KEY_DIM)
            vs = slice(h * RET_VAL_DIM, (h + 1) * RET_VAL_DIM)
            qcol = _column(q_ref[r, :, ks], RET_KEY_DIM)
            kcol = _column(k_ref[r, :, ks], RET_KEY_DIM)
            v = v_ref[r, :, vs]
            s_new = s_ref[0, r, h] * gam_ref[h] + kcol * v
            so_ref[0, r, h] = s_new
            o = jnp.sum(qcol * s_new, axis=0, keepdims=True)
            o_ref[r, :, vs] = _group_norm_gate(o, ng_ref[:, vs], g_ref[r, :, vs])


def _ret_core_sample(q, k, v, g, state_ret, j, norm_g, new_states):
    db = q.shape[0]
    lg = jnp.log1p(-jnp.exp2(-5.0 - jnp.arange(RET_HEADS, dtype=F32)))
    gam = jnp.broadcast_to(jnp.exp(lg)[:, None, None], (RET_HEADS, 1, RET_VAL_DIM))
    tb = 4 if db % 4 == 0 else 1
    kspec = pl.BlockSpec((tb, 1, RET_HEADS * RET_KEY_DIM), lambda bi: (bi, 0, 0))
    vspec = pl.BlockSpec((tb, 1, RET_HEADS * RET_VAL_DIM), lambda bi: (bi, 0, 0))
    sspec = pl.BlockSpec((1, tb, RET_HEADS, RET_KEY_DIM, RET_VAL_DIM), lambda bi: (j, bi, 0, 0, 0))
    in_specs = [kspec, kspec, vspec, vspec, sspec, _resident(gam.shape), _resident(norm_g.shape)]
    args = [q, k, v, g, state_ret, gam, norm_g]
    aliases = {}
    if new_states is not None:
        in_specs.append(pl.BlockSpec(memory_space=pl.ANY))
        args.append(new_states)
        aliases = {len(args) - 1: 1}
    return pl.pallas_call(
        _ret_sample_kernel,
        grid=(db // tb,),
        in_specs=in_specs,
        out_specs=[vspec, sspec],
        out_shape=[jax.ShapeDtypeStruct((db, 1, RET_HEADS * RET_VAL_DIM), F32),
                   jax.ShapeDtypeStruct(state_ret.shape, F32)],
        input_output_aliases=aliases,
        compiler_params=_params("parallel"),
        name="ret_sample",
    )(*args)


POOL_HALO = 16


def _pool_prompt_kernel(x_ref, xh_ref, w_ref, sc_ref, g_ref, b_ref, o_ref, ext_ref, z_ref):
    i = pl.program_id(1)
    tm = x_ref.shape[1]
    x = x_ref[0]
    halo = jnp.where(i == 0, 0.0, xh_ref[0])
    ext_ref[0:POOL_HALO, :] = halo
    ext_ref[POOL_HALO:POOL_HALO + tm, :] = x
    pos = i * tm + lax.broadcasted_iota(I32, (tm, 1), 0)
    for gi, w in enumerate(POOL_WINDOWS):
        cs = slice(gi * POOL_GROUP_DIM, (gi + 1) * POOL_GROUP_DIM)
        wsum = ext_ref[POOL_HALO:POOL_HALO + tm, cs]
        for jj in range(1, w):
            wsum = wsum + ext_ref[POOL_HALO - jj:POOL_HALO - jj + tm, cs]
        cnt = jnp.minimum(w, pos + 1).astype(F32)
        d = wsum / cnt - x[:, cs]
        y = _dot(d.astype(BF16), w_ref[gi]) * sc_ref[:, cs]
        z_ref[:, cs] = DEEPNORM_ALPHA * x[:, cs] + y
    o_ref[0] = _layer_norm(z_ref[...], g_ref[...], b_ref[...])


def _pool_layer_prompt(x, w_grp, scale, g, b):
    bsz, l, _ = x.shape
    tm = _row_tile(l)
    hb = tm // POOL_HALO
    return pl.pallas_call(
        _pool_prompt_kernel,
        grid=(bsz, l // tm),
        in_specs=[pl.BlockSpec((1, tm, D_MODEL), lambda bi, i: (bi, i, 0)),
                  pl.BlockSpec((1, POOL_HALO, D_MODEL), lambda bi, i: (bi, jnp.maximum(i * hb - 1, 0), 0)),
                  _resident(w_grp.shape), _resident((1, D_MODEL)), _resident((1, D_MODEL)),
                  _resident((1, D_MODEL))],
        out_specs=pl.BlockSpec((1, tm, D_MODEL), lambda bi, i: (bi, i, 0)),
        out_shape=jax.ShapeDtypeStruct((bsz, l, D_MODEL), F32),
        scratch_shapes=[pltpu.VMEM((POOL_HALO + tm, D_MODEL), F32), pltpu.VMEM((tm, D_MODEL), F32)],
        compiler_params=_params("parallel", "arbitrary"),
        name="pool_prompt",
    )(x, x, w_grp, scale, g, b)


def _pool_sample_kernel(x_ref, h_ref, w_ref, sc_ref, g_ref, b_ref, o_ref, z_ref):
    x = x_ref[...]
    for gi, w in enumerate(POOL_WINDOWS):
        cs = slice(gi * POOL_GROUP_DIM, (gi + 1) * POOL_GROUP_DIM)
        wsum = x[:, cs]
        for jj in range(1, w):
            wsum = wsum + h_ref[POOL_HIST - jj, :, cs]
        d = wsum / float(w) - x[:, cs]
        y = _dot(d.astype(BF16), w_ref[gi]) * sc_ref[:, cs]
        z_ref[:, cs] = DEEPNORM_ALPHA * x[:, cs] + y
    o_ref[...] = _layer_norm(z_ref[...], g_ref[...], b_ref[...])


def _pool_layer_sample(x, hist_t, w_grp, scale, g, b):
    db = x.shape[0]
    return pl.pallas_call(
        _pool_sample_kernel,
        grid=(1,),
        in_specs=[_resident(x.shape), _resident(hist_t.shape), _resident(w_grp.shape),
                  _resident((1, D_MODEL)), _resident((1, D_MODEL)), _resident((1, D_MODEL))],
        out_specs=pl.BlockSpec((db, D_MODEL), lambda i: (0, 0)),
        out_shape=jax.ShapeDtypeStruct((db, D_MODEL), F32),
        scratch_shapes=[pltpu.VMEM((db, D_MODEL), F32)],
        compiler_params=_params("arbitrary"),
        name="pool_sample",
    )(x, hist_t, w_grp, scale, g, b)


DSA_QKV = 3 * D_MODEL
DSA_QI_PAD = IDX_HEADS * LANES
DSA_PAD_IN = DSA_QKV + DSA_QI_PAD + LANES
WI_LANE = IDX_DIM


def _dsa_pad_w_in(w_in):
    d = w_in.shape[0]
    qi0 = DSA_QKV
    ki0 = qi0 + IDX_HEADS * IDX_DIM
    z = jnp.zeros((d, LANES - IDX_DIM), w_in.dtype)
    parts = [w_in[:, :DSA_QKV]]
    for h in range(IDX_HEADS):
        parts += [w_in[:, qi0 + h * IDX_DIM:qi0 + (h + 1) * IDX_DIM], z]
    parts += [w_in[:, ki0:], jnp.zeros((d, LANES - IDX_DIM - IDX_HEADS), w_in.dtype)]
    return jnp.concatenate(parts, axis=1)


def _rope_rolled(y, cos, sin, half):
    if 2 * half == LANES:
        rot = pltpu.roll(y, half, 1)
    else:
        lane = lax.broadcasted_iota(I32, y.shape, 1)
        rot = jnp.where(lane < half, pltpu.roll(y, LANES - half, 1), pltpu.roll(y, half, 1))
    return y * cos + rot * sin


def _dsa_proj_kernel(x_ref, w_ref, cq_ref, sq_ref, ci_ref, si_ref,
                     qb_ref, k_ref, v_ref, kb_ref, vb_ref, qi_ref, kiw_ref):
    xb = x_ref[...].astype(BF16)
    cq = cq_ref[...]
    sq = sq_ref[...]
    ci = ci_ref[...]
    si = si_ref[...]
    hd = ATTN_HEAD_DIM
    wide = 4 * hd
    for c in range(0, D_MODEL, wide):
        yq = _dot(xb, w_ref[:, c:c + wide])
        yk = _dot(xb, w_ref[:, D_MODEL + c:D_MODEL + c + wide])
        yv = _dot(xb, w_ref[:, 2 * D_MODEL + c:2 * D_MODEL + c + wide])
        v_ref[:, c:c + wide] = yv
        vb_ref[:, c:c + wide] = yv.astype(BF16)
        for s in range(0, wide, hd):
            cs = slice(c + s, c + s + hd)
            qb_ref[:, cs] = _rope_rolled(yq[:, s:s + hd], cq, sq, hd // 2).astype(BF16)
            kr = _rope_rolled(yk[:, s:s + hd], cq, sq, hd // 2)
            k_ref[:, cs] = kr
            kb_ref[:, cs] = kr.astype(BF16)
    yi = _dot(xb, w_ref[:, DSA_QKV:DSA_QKV + DSA_QI_PAD + LANES])
    for h in range(IDX_HEADS):
        qi_ref[:, h * LANES:(h + 1) * LANES] = _rope_rolled(yi[:, h * LANES:(h + 1) * LANES], ci, si, IDX_DIM // 2)
    kiw_ref[...] = _rope_rolled(yi[:, DSA_QI_PAD:], ci, si, IDX_DIM // 2)


def _dsa_proj(x, w_pad, tabs, pos_tiles):
    m = x.shape[0]
    tm = _row_tile(m)
    tab = pl.BlockSpec((tm, LANES), lambda i: (i % pos_tiles, 0))
    row = lambda n: pl.BlockSpec((tm, n), lambda i: (i, 0))
    sds = lambda n, dt: jax.ShapeDtypeStruct((m, n), dt)
    return pl.pallas_call(
        _dsa_proj_kernel,
        grid=(m // tm,),
        in_specs=[row(D_MODEL), _resident(w_pad.shape), tab, tab, tab, tab],
        out_specs=[row(D_MODEL), row(D_MODEL), row(D_MODEL), row(D_MODEL), row(D_MODEL),
                   row(DSA_QI_PAD), row(LANES)],
        out_shape=[sds(D_MODEL, BF16), sds(D_MODEL, F32), sds(D_MODEL, F32), sds(D_MODEL, BF16),
                   sds(D_MODEL, BF16), sds(DSA_QI_PAD, F32), sds(LANES, F32)],
        compiler_params=_params("parallel"),
        name="dsa_proj",
    )(x, w_pad, *tabs)


F32_ROWS = 8
I16_ROWS = 16


def _topk_bias_t(score, kidx, limit, n_top, idx_bits, sc_ref, eq_ref):
    n, q = score.shape
    vis = kidx <= jnp.broadcast_to(jnp.asarray(limit, I32), (n, q))
    sc_ref[0:n, :] = jnp.where(vis, score, NEG_INF)

    def count(pick, slabs):
        part = pick(0)
        for s in range(1, slabs):
            part = part + pick(s)
        return jnp.sum(part.astype(I32), axis=0, keepdims=True)

    def as_float(key):
        key = jnp.maximum(key, INT32_MIN + 0x7FFFFF)
        return lax.bitcast_convert_type(key ^ ((key >> 31) & 0x7FFFFFFF), F32)

    def count_ge(key):
        c = jnp.broadcast_to(as_float(key), (F32_ROWS, q))
        return count(lambda s: jnp.where(sc_ref[s * F32_ROWS:(s + 1) * F32_ROWS, :] >= c, 1, 0),
                     n // F32_ROWS)

    z = jnp.zeros((1, q), I32)
    t = jnp.where(count_ge(z) >= n_top, z, INT32_MIN)

    def value_bit(i, t):
        cand = t + jnp.left_shift(jnp.int32(1), 30 - i)
        return jnp.where(count_ge(cand) >= n_top, cand, t)

    t = jnp.broadcast_to(as_float(lax.fori_loop(0, 31, value_bit, t)), (n, q))
    sc = sc_ref[0:n, :]
    gt = sc > t
    eq = sc == t
    gt_i = jnp.where(gt, 1, 0)
    need = n_top - count(lambda s: gt_i[s * F32_ROWS:(s + 1) * F32_ROWS, :], n // F32_ROWS)
    eq_ref[0:n, :] = jnp.where(eq, 1, 0).astype(I16)
    zero = jnp.zeros((), I16)
    sub = lax.broadcasted_iota(I16, (I16_ROWS, q), 0)

    def index_bit(i, j):
        cand = j + jnp.left_shift(jnp.int32(1), idx_bits - 1 - i)
        c16 = jnp.broadcast_to(cand.astype(I16), (I16_ROWS, q))
        before = count(lambda s: jnp.where(sub + jnp.int16(s * I16_ROWS) < c16,
                                           eq_ref[s * I16_ROWS:(s + 1) * I16_ROWS, :], zero), n // I16_ROWS)
        return jnp.where(before < need, cand, j)

    j = lax.fori_loop(0, idx_bits, index_bit, z)
    picked = jnp.where(gt, 0.0, jnp.where(eq, jnp.where(kidx <= jnp.broadcast_to(j, (n, q)), 0.0, NEG_INF), NEG_INF))
    return jnp.where(vis, picked, NEG_INF)


def _stack_idx_heads(qi):
    return jnp.concatenate([qi[:, h * LANES:h * LANES + IDX_DIM] for h in range(IDX_HEADS)],
                           axis=0).astype(BF16)


def _dsa_attn_prompt_kernel(q_ref, qi_ref, wt_ref, kiw_ref, kb_ref, vb_ref, o_ref,
                            sc_ref, eq_ref, *, n_top, widths):
    i = pl.program_id(1)
    tq = q_ref.shape[1]
    hd = ATTN_HEAD_DIM

    def attend(n):
        r = _dot_nt(kiw_ref[0, :n, :IDX_DIM].astype(BF16), _stack_idx_heads(qi_ref[0]))
        score = jnp.zeros((n, tq), F32)
        for h in range(IDX_HEADS):
            score = score + (wt_ref[0, h:h + 1, :] * IDX_SCALE) * jnp.maximum(r[:, h * tq:(h + 1) * tq], 0.0)
        kidx = lax.broadcasted_iota(I32, (n, tq), 0)
        pos = i * tq + lax.broadcasted_iota(I32, (1, tq), 1)
        bias = _topk_bias_t(score, kidx, pos, n_top, max(1, (n - 1).bit_length()),
                            sc_ref, eq_ref).T
        for h in range(ATTN_HEADS):
            cs = slice(h * hd, (h + 1) * hd)
            s = _dot_nt(q_ref[0, :, cs], kb_ref[0, :n, cs]) * (hd ** -0.5) + bias
            m = jnp.max(s, axis=-1, keepdims=True)
            p = jnp.exp(s - m)
            l = jnp.sum(p, axis=-1, keepdims=True)
            o_ref[0, :, cs] = (_dot(p.astype(BF16), vb_ref[0, :n, cs]) / l).astype(o_ref.dtype)

    last = (i + 1) * tq
    prev = 0
    for n in widths:
        pl.when(jnp.logical_and(last > prev, last <= n))(functools.partial(attend, n))
        prev = n


def _dsa_attn_prompt(qb, qi, kiw, kb, vb, n_top):
    b, l, _ = qb.shape
    tq = 256 if l % 256 == 0 else 128
    step = 2 * tq if l % (2 * tq) == 0 else tq
    widths = tuple(range(step, l + 1, step))
    wt = jnp.swapaxes(kiw[:, :, WI_LANE:WI_LANE + IDX_HEADS], 1, 2)
    qrow = lambda n: pl.BlockSpec((1, tq, n), lambda bi, i: (bi, i, 0))
    full = lambda n: pl.BlockSpec((1, l, n), lambda bi, i: (bi, 0, 0))
    return pl.pallas_call(
        functools.partial(_dsa_attn_prompt_kernel, n_top=n_top, widths=widths),
        grid=(b, l // tq),
        in_specs=[qrow(D_MODEL), qrow(DSA_QI_PAD), pl.BlockSpec((1, IDX_HEADS, tq), lambda bi, i: (bi, 0, i)),
                  full(LANES), full(D_MODEL), full(D_MODEL)],
        out_specs=qrow(D_MODEL),
        out_shape=jax.ShapeDtypeStruct((b, l, D_MODEL), BF16),
        scratch_shapes=[pltpu.VMEM((l, tq), F32), pltpu.VMEM((l, tq), I16)],
        compiler_params=_params("parallel", "arbitrary"),
        name="dsa_attn_prompt",
    )(qb, qi, wt, kiw, kb, vb)


def _dsa_idx_sample_kernel(pt_ref, qi_ref, kiwq_ref, *rest):
    del pt_ref
    pages, o_ref = rest[:-1], rest[-1]
    qs = _stack_idx_heads(qi_ref[0])
    wq = kiwq_ref[0]
    rows = wq.shape[0]
    for p, page_ref in enumerate(pages):
        r = _dot(qs, page_ref[0].astype(BF16))
        acc = jnp.zeros((rows, PAGE_SIZE), F32)
        for h in range(IDX_HEADS):
            acc = acc + (wq[:, WI_LANE + h:WI_LANE + h + 1] * IDX_SCALE) * jnp.maximum(
                r[h * rows:(h + 1) * rows], 0.0)
        o_ref[0, :, p * PAGE_SIZE:(p + 1) * PAGE_SIZE] = acc


def _dsa_idx_sample(page_table, qi, kiw, cache_kidx_t):
    db, n_pages = page_table.shape
    qi8 = jnp.broadcast_to(qi[:, None, :], (db, 8, DSA_QI_PAD))
    kiw8 = jnp.broadcast_to(kiw[:, None, :], (db, 8, LANES))
    page_specs = [pl.BlockSpec((1, IDX_DIM, PAGE_SIZE), functools.partial(lambda p, bi, pt: (pt[bi, p], 0, 0), p))
                  for p in range(n_pages)]
    out = pl.pallas_call(
        _dsa_idx_sample_kernel,
        grid_spec=pltpu.PrefetchScalarGridSpec(
            num_scalar_prefetch=1,
            grid=(db,),
            in_specs=[pl.BlockSpec((1, 8, DSA_QI_PAD), lambda bi, pt: (bi, 0, 0)),
                      pl.BlockSpec((1, 8, LANES), lambda bi, pt: (bi, 0, 0))] + page_specs,
            out_specs=pl.BlockSpec((1, 8, n_pages * PAGE_SIZE), lambda bi, pt: (bi, 0, 0)),
        ),
        out_shape=jax.ShapeDtypeStruct((db, 8, n_pages * PAGE_SIZE), F32),
        compiler_params=_params("arbitrary"),
        name="dsa_idx_sample",
    )(page_table, qi8, kiw8, *([cache_kidx_t] * n_pages))
    return out[:, 0, :]


def _dsa_sel_sample_kernel(spt_ref, qi_ref, kiw_ref, o_ref, sc_ref, eq_ref, *, past, n_top, idx_bits):
    db = qi_ref.shape[0]
    kiw = kiw_ref[...]
    lane = lax.broadcasted_iota(I32, kiw.shape, 1)
    ki = jnp.where(lane < IDX_DIM, kiw, 0.0)
    qi = qi_ref[...]
    self_score = jnp.zeros((db, 1), F32)
    for h in range(IDX_HEADS):
        r = jnp.sum(qi[:, h * LANES:(h + 1) * LANES] * ki, axis=-1, keepdims=True)
        self_score = self_score + (kiw[:, WI_LANE + h:WI_LANE + h + 1] * IDX_SCALE) * jnp.maximum(r, 0.0)
    self_rows = jnp.broadcast_to(self_score, (db, LANES)).T[0:I16_ROWS, :]
    n = past + I16_ROWS
    score = jnp.concatenate([spt_ref[...], self_rows], axis=0)
    kidx = lax.broadcasted_iota(I32, (n, db), 0)
    o_ref[...] = _topk_bias_t(score, kidx, past, n_top, idx_bits, sc_ref, eq_ref)


def _dsa_sel_sample(score_past, qi, kiw, n_top):
    db, past = score_past.shape
    n = past + I16_ROWS
    spt = score_past.T
    out = pl.pallas_call(
        functools.partial(_dsa_sel_sample_kernel, past=past, n_top=n_top, idx_bits=n.bit_length()),
        grid=(1,),
        in_specs=[_resident(spt.shape), _resident(qi.shape), _resident(kiw.shape)],
        out_specs=pl.BlockSpec((n, db), lambda i: (0, 0)),
        out_shape=jax.ShapeDtypeStruct((n, db), F32),
        scratch_shapes=[pltpu.VMEM((n, db), F32), pltpu.VMEM((n, db), I16)],
        compiler_params=_params("arbitrary"),
        name="dsa_sel_sample",
    )(spt, qi, kiw)
    return out.T


def _dsa_attn_sample_kernel(pt_ref, q_ref, kn_ref, vn_ref, bias_ref, bself_ref, *rest, n_pages):
    del pt_ref
    kpages, vpages, o_ref = rest[:n_pages], rest[n_pages:2 * n_pages], rest[-1]
    hd = ATTN_HEAD_DIM
    rows = PAGE_SIZE * ATTN_HEADS
    qb = q_ref[0]
    scale = hd ** -0.5
    head = lax.broadcasted_iota(I32, (ATTN_HEADS, rows), 0)
    col = lax.broadcasted_iota(I32, (ATTN_HEADS, rows), 1)
    own = jnp.bitwise_and(col, ATTN_HEADS - 1) == head
    s_pages = []
    for p, kp in enumerate(kpages):
        k2 = kp[0].reshape(rows, hd).astype(BF16)
        s = _dot_nt(qb, k2) * scale + bias_ref[0, :, p * rows:(p + 1) * rows]
        s_pages.append(jnp.where(own, s, NEG_INF))
    s_self = jnp.sum(qb.astype(F32) * kn_ref[0].astype(BF16).astype(F32), axis=-1, keepdims=True) * scale
    s_self = s_self + bself_ref[0, :, 0:1]
    m = s_self
    for s in s_pages:
        m = jnp.maximum(m, jnp.max(s, axis=-1, keepdims=True))
    p_self = jnp.exp(s_self - m)
    l = p_self
    acc = p_self.astype(BF16).astype(F32) * vn_ref[0].astype(BF16).astype(F32)
    for s, vp in zip(s_pages, vpages):
        pexp = jnp.exp(s - m)
        l = l + jnp.sum(pexp, axis=-1, keepdims=True)
        acc = acc + _dot(pexp.astype(BF16), vp[0].reshape(rows, hd).astype(BF16))
    o_ref[0] = acc / l


def _dsa_attn_sample(page_table, q, k_new, v_new, bias, cache_k, cache_v):
    db, n_pages = page_table.shape
    past = n_pages * PAGE_SIZE
    bias_keys = jnp.repeat(bias[:, :past], ATTN_HEADS, axis=1)[:, None, :]
    bias_self = bias[:, None, past:]
    row = pl.BlockSpec((1, ATTN_HEADS, ATTN_HEAD_DIM), lambda bi, pt: (bi, 0, 0))
    page_spec = lambda p: pl.BlockSpec((1, PAGE_SIZE, ATTN_HEADS, ATTN_HEAD_DIM),
                                       functools.partial(lambda p, bi, pt: (pt[bi, p], 0, 0, 0), p))
    return pl.pallas_call(
        functools.partial(_dsa_attn_sample_kernel, n_pages=n_pages),
        grid_spec=pltpu.PrefetchScalarGridSpec(
            num_scalar_prefetch=1,
            grid=(db,),
            in_specs=[row, row, row,
                      pl.BlockSpec((1, 1, past * ATTN_HEADS), lambda bi, pt: (bi, 0, 0)),
                      pl.BlockSpec((1, 1, bias_self.shape[-1]), lambda bi, pt: (bi, 0, 0))]
                     + [page_spec(p) for p in range(n_pages)] + [page_spec(p) for p in range(n_pages)],
            out_specs=row,
        ),
        out_shape=jax.ShapeDtypeStruct((db, ATTN_HEADS, ATTN_HEAD_DIM), F32),
        compiler_params=_params("arbitrary"),
        name="dsa_attn_sample",
    )(page_table, q, k_new, v_new, bias_keys, bias_self, *([cache_k] * n_pages), *([cache_v] * n_pages))


def kernel(x_prompt, x_sample, state_ret, state_pool, cache_k, cache_v, cache_kidx, page_table,
           ret_w_in, ret_norm_g, ret_w_out, pool_w_grp, pool_scale, dsa_w_in, dsa_w_out,
           ffn_w_in, ffn_w_out, ln_g, ln_b):
    bp, lp, d = x_prompt.shape
    db, ls, _ = x_sample.shape
    assert d == D_MODEL and ls == 1
    past = page_table.shape[1] * PAGE_SIZE
    mp = bp * lp
    tmp = _row_tile(mp)
    assert lp % tmp == 0
    xp = x_prompt.reshape(mp, d)
    xs = x_sample.reshape(db, d)
    pos_p = jnp.arange(lp)
    pos_s = jnp.full((db,), past, jnp.int32)

    ret_w_in_b, ret_w_out_b = ret_w_in.astype(BF16), ret_w_out.astype(BF16)
    ffn_w_in_b, ffn_w_out_b = ffn_w_in.astype(BF16), ffn_w_out.astype(BF16)
    dsa_w_out_b = dsa_w_out.astype(BF16)
    ret_p, pool_p, pool_s = [], [], []
    ret_s = None
    kp_l, vp_l, kip_l, ks_l, vs_l, kis_l = [], [], [], [], [], []
    for i in range(DEPTH):
        kind, j = i % 3, i // 3
        g1, b1 = ln_g[i, 0][None, :], ln_b[i, 0][None, :]
        g2, b2 = ln_g[i, 1][None, :], ln_b[i, 1][None, :]
        if kind == 0:
            ng = ret_norm_g[j][None, :]
            half = RET_KEY_DIM // 2
            cos_p, sin_p = _rope_tables_split(pos_p, half)
            cos_s, sin_s = _rope_tables_split(pos_s, half)
            q, k, v, g = _ret_proj(xp, ret_w_in_b, j, cos_p, sin_p, lp // tmp, BF16)
            shp = lambda a: a.reshape(bp, lp, a.shape[-1])
            o, sp = _ret_core_prompt(shp(q), shp(k), shp(v), shp(g), ng)
            xp = _out_ln(o.reshape(mp, -1), ret_w_out_b, j, xp, g1, b1)
            q, k, v, g = _ret_proj(xs, ret_w_in_b, j, cos_s, sin_s, 1, F32)
            o, ret_s = _ret_core_sample(q[:, None, :], k[:, None, :], v[:, None, :], g[:, None, :],
                                        state_ret, j, ng, ret_s)
            xs = _out_ln(o.reshape(db, -1), ret_w_out_b, j, xs, g1, b1)
            ret_p.append(sp)
        elif kind == 1:
            w_grp = pool_w_grp[j].astype(BF16)
            sc = pool_scale[j][None, :]
            pool_p.append(xp.reshape(bp, lp, d)[:, -POOL_HIST:])
            pool_s.append(jnp.concatenate([state_pool[j][:, 1:], xs[:, None, :]], axis=1))
            xp = _pool_layer_prompt(xp.reshape(bp, lp, d), w_grp, sc, g1, b1).reshape(mp, d)
            xs = _pool_layer_sample(xs, jnp.swapaxes(state_pool[j], 0, 1), w_grp, sc, g1, b1)
        else:
            w_pad = _dsa_pad_w_in(dsa_w_in[j]).astype(BF16)
            tabs_p = _rope_tables_rolled(pos_p, ATTN_HEAD_DIM // 2) + _rope_tables_rolled(pos_p, IDX_DIM // 2)
            tabs_s = _rope_tables_rolled(pos_s, ATTN_HEAD_DIM // 2) + _rope_tables_rolled(pos_s, IDX_DIM // 2)
            qb, k32, v32, kb, vb, qi, kiw = _dsa_proj(xp, w_pad, tabs_p, lp // tmp)
            shp = lambda a: a.reshape(bp, lp, a.shape[-1])
            o = _dsa_attn_prompt(shp(qb), shp(qi), shp(kiw), shp(kb), shp(vb), min(TOPK_MAX, lp // 4))
            xp = _out_ln(o.reshape(mp, d), dsa_w_out_b, j, xp, g1, b1)
            kp_l.append(k32.reshape(bp, lp, ATTN_HEADS, ATTN_HEAD_DIM))
            vp_l.append(v32.reshape(bp, lp, ATTN_HEADS, ATTN_HEAD_DIM))
            kip_l.append(kiw[:, :IDX_DIM].reshape(bp, lp, IDX_DIM))

            qb, k32, v32, kb, vb, qi, kiw = _dsa_proj(xs, w_pad, tabs_s, 1)
            score_past = _dsa_idx_sample(page_table, qi, kiw, jnp.swapaxes(cache_kidx[j], -1, -2))
            bias = _dsa_sel_sample(score_past, qi, kiw, min(TOPK_MAX, (past + ls) // 4))
            heads = lambda a: a.reshape(db, ATTN_HEADS, ATTN_HEAD_DIM)
            o = _dsa_attn_sample(page_table, heads(qb), heads(k32), heads(v32), bias, cache_k[j], cache_v[j])
            xs = _out_ln(o.reshape(db, d), dsa_w_out_b, j, xs, g1, b1)
            ks_l.append(k32.reshape(db, ls, ATTN_HEADS, ATTN_HEAD_DIM))
            vs_l.append(v32.reshape(db, ls, ATTN_HEADS, ATTN_HEAD_DIM))
            kis_l.append(kiw[:, :IDX_DIM].reshape(db, ls, IDX_DIM))
        xp = _ffn_ln(xp, ffn_w_in_b, ffn_w_out_b, i, g2, b2)
        xs = _ffn_ln(xs, ffn_w_in_b, ffn_w_out_b, i, g2, b2)
    return (xp.reshape(bp, lp, d), xs.reshape(db, ls, d), jnp.stack(ret_p), ret_s,
            jnp.stack(pool_p), jnp.stack(pool_s), jnp.stack(kp_l), jnp.stack(vp_l), jnp.stack(kip_l),
            jnp.stack(ks_l), jnp.stack(vs_l), jnp.stack(kis_l))
```

```python
import functools
import math

import jax
import jax.numpy as jnp
from jax import lax
from jax.experimental import pallas as pl
from jax.experimental.pallas import tpu as pltpu

F32 = jnp.float32
BF16 = jnp.bfloat16
I32 = jnp.int32
I16 = jnp.int16

LANES = 128
VMEM_LIMIT_BYTES = 56 * 1024 * 1024

D_MODEL = 1024
DEPTH = 4
PAGE_SIZE = 128
RET_HEADS = 4
RET_KEY_DIM = D_MODEL // RET_HEADS
RET_VAL_DIM = 2 * RET_KEY_DIM
RET_BLOCK = 256
POOL_WINDOWS = (2, 4, 8, 16)
POOL_GROUP_DIM = D_MODEL // len(POOL_WINDOWS)
POOL_HIST = max(POOL_WINDOWS) - 1
ATTN_HEADS = 8
ATTN_HEAD_DIM = D_MODEL // ATTN_HEADS
IDX_HEADS = 4
IDX_DIM = 64
IDX_SCALE = (IDX_HEADS ** -0.5) * (IDX_DIM ** -0.5)
TOPK_MAX = 256
FFN_DIM = -(-8 * D_MODEL // (3 * 256)) * 256
ROPE_THETA = 10000.0
LN_EPS = 1e-5
DEEPNORM_ALPHA = (2.0 * DEPTH) ** 0.25
INT32_MIN = -(2 ** 31)
NEG_INF = float("-inf")


def _params(*sem):
    return pltpu.CompilerParams(dimension_semantics=sem, vmem_limit_bytes=VMEM_LIMIT_BYTES)


def _resident(shape):
    nd = len(shape)
    return pl.BlockSpec(shape, lambda *_: (0,) * nd, pipeline_mode=pl.Buffered(1))


def _resident_layer(stacked, layer):
    shape = stacked.shape[1:]
    nd = len(shape)
    return pl.BlockSpec((None,) + shape, lambda *_: (layer,) + (0,) * nd, pipeline_mode=pl.Buffered(1))


def _row_tile(m):
    for t in (512, 256, 128):
        if m % t == 0:
            return t
    raise ValueError(f"token count {m} is not a multiple of 128")


def _layer_norm(z, g, b):
    mu = jnp.mean(z, axis=-1, keepdims=True)
    zc = z - mu
    var = jnp.mean(zc * zc, axis=-1, keepdims=True)
    return zc * lax.rsqrt(var + LN_EPS) * g + b


def _dot(a, b):
    return jnp.dot(a, b, preferred_element_type=F32)


def _dot_nt(a, b, precision=None):
    return lax.dot_general(a, b, (((1,), (1,)), ((), ())), precision=precision,
                           preferred_element_type=F32)


def _rope_angles(pos, half):
    inv = jnp.power(ROPE_THETA, -jnp.arange(half, dtype=F32) / half)
    return pos.astype(F32)[:, None] * inv[None, :]


def _rope_tables_split(pos, half):
    ang = _rope_angles(pos, half)
    return jnp.cos(ang), jnp.sin(ang)


def _rope_tables_rolled(pos, half, width=LANES):
    ang = _rope_angles(pos, half)
    cos = jnp.concatenate([jnp.cos(ang), jnp.cos(ang)], axis=-1)
    sin = jnp.concatenate([-jnp.sin(ang), jnp.sin(ang)], axis=-1)
    pad = width - 2 * half
    if pad:
        cos = jnp.concatenate([cos, jnp.ones((pos.shape[0], pad), F32)], axis=-1)
        sin = jnp.concatenate([sin, jnp.zeros((pos.shape[0], pad), F32)], axis=-1)
    return cos, sin


def _ret_proj_kernel(x_ref, w_ref, cos_ref, sin_ref, q_ref, k_ref, v_ref, g_ref):
    xb = x_ref[...].astype(BF16)
    cos = cos_ref[...]
    sin = sin_ref[...]
    hk = RET_HEADS * RET_KEY_DIM
    hv = RET_HEADS * RET_VAL_DIM
    half = RET_KEY_DIM // 2
    for h in range(RET_HEADS):
        for base, o_ref, scale in ((0, q_ref, None), (hk, k_ref, RET_KEY_DIM ** -0.5)):
            c0 = base + h * RET_KEY_DIM
            y = _dot(xb, w_ref[:, c0:c0 + RET_KEY_DIM])
            x1 = y[:, :half]
            x2 = y[:, half:]
            o1 = x1 * cos - x2 * sin
            o2 = x2 * cos + x1 * sin
            if scale is not None:
                o1 = o1 * scale
                o2 = o2 * scale
            o_ref[:, h * RET_KEY_DIM:h * RET_KEY_DIM + half] = o1.astype(o_ref.dtype)
            o_ref[:, h * RET_KEY_DIM + half:(h + 1) * RET_KEY_DIM] = o2.astype(o_ref.dtype)
    for h in range(RET_HEADS):
        c0 = h * RET_VAL_DIM
        v_ref[:, c0:c0 + RET_VAL_DIM] = _dot(
            xb, w_ref[:, 2 * hk + c0:2 * hk + c0 + RET_VAL_DIM]).astype(v_ref.dtype)
        g_ref[:, c0:c0 + RET_VAL_DIM] = _dot(
            xb, w_ref[:, 2 * hk + hv + c0:2 * hk + hv + c0 + RET_VAL_DIM]).astype(g_ref.dtype)


def _ret_proj(x, w, layer, cos, sin, pos_tiles, out_dtype):
    m = x.shape[0]
    tm = _row_tile(m)
    hk = RET_HEADS * RET_KEY_DIM
    hv = RET_HEADS * RET_VAL_DIM
    half = RET_KEY_DIM // 2
    tab = pl.BlockSpec((tm, half), lambda i: (i % pos_tiles, 0))
    return pl.pallas_call(
        _ret_proj_kernel,
        grid=(m // tm,),
        in_specs=[pl.BlockSpec((tm, D_MODEL), lambda i: (i, 0)), _resident_layer(w, layer), tab, tab],
        out_specs=[pl.BlockSpec((tm, hk), lambda i: (i, 0)), pl.BlockSpec((tm, hk), lambda i: (i, 0)),
                   pl.BlockSpec((tm, hv), lambda i: (i, 0)), pl.BlockSpec((tm, hv), lambda i: (i, 0))],
        out_shape=[jax.ShapeDtypeStruct((m, hk), out_dtype), jax.ShapeDtypeStruct((m, hk), out_dtype),
                   jax.ShapeDtypeStruct((m, hv), out_dtype), jax.ShapeDtypeStruct((m, hv), out_dtype)],
        compiler_params=_params("parallel"),
        name="ret_proj",
    )(x, w, cos, sin)


def _out_ln_kernel(a_ref, w_ref, x_ref, g_ref, b_ref, o_ref):
    y = _dot(a_ref[...].astype(BF16), w_ref[...])
    o_ref[...] = _layer_norm(DEEPNORM_ALPHA * x_ref[...] + y, g_ref[...], b_ref[...])


def _out_ln(a, w, layer, x, g, b):
    m, ka = a.shape
    tm = _row_tile(m)
    return pl.pallas_call(
        _out_ln_kernel,
        grid=(m // tm,),
        in_specs=[pl.BlockSpec((tm, ka), lambda i: (i, 0)), _resident_layer(w, layer),
                  pl.BlockSpec((tm, D_MODEL), lambda i: (i, 0)),
                  _resident((1, D_MODEL)), _resident((1, D_MODEL))],
        out_specs=pl.BlockSpec((tm, D_MODEL), lambda i: (i, 0)),
        out_shape=jax.ShapeDtypeStruct((m, D_MODEL), F32),
        compiler_params=_params("parallel"),
        name="out_ln",
    )(a, w, x, g, b)


FFN_CHUNK = 256


def _ffn_kernel(x_ref, w_in_ref, w_out_ref, g_ref, b_ref, o_ref, acc_ref):
    x = x_ref[...]
    xb = x.astype(BF16)
    for n, c in enumerate(range(0, FFN_DIM, FFN_CHUNK)):
        a = _dot(xb, w_in_ref[:, c:c + FFN_CHUNK])
        u = _dot(xb, w_in_ref[:, FFN_DIM + c:FFN_DIM + c + FFN_CHUNK])
        hact = (a * jax.nn.sigmoid(a) * u).astype(BF16)
        y = _dot(hact, w_out_ref[c:c + FFN_CHUNK, :])
        if n == 0:
            acc_ref[...] = y
        else:
            acc_ref[...] += y
    o_ref[...] = _layer_norm(DEEPNORM_ALPHA * x + acc_ref[...], g_ref[...], b_ref[...])


def _ffn_ln(x, w_in, w_out, layer, g, b):
    m = x.shape[0]
    tm = _row_tile(m)
    return pl.pallas_call(
        _ffn_kernel,
        grid=(m // tm,),
        in_specs=[pl.BlockSpec((tm, D_MODEL), lambda i: (i, 0)), _resident_layer(w_in, layer),
                  _resident_layer(w_out, layer), _resident((1, D_MODEL)), _resident((1, D_MODEL))],
        out_specs=pl.BlockSpec((tm, D_MODEL), lambda i: (i, 0)),
        out_shape=jax.ShapeDtypeStruct((m, D_MODEL), F32),
        scratch_shapes=[pltpu.VMEM((tm, D_MODEL), F32)],
        compiler_params=_params("parallel"),
        name="ffn_ln",
    )(x, w_in, w_out, g, b)


def _group_norm_gate(o, norm_g, gate):
    mu = jnp.mean(o, axis=-1, keepdims=True)
    oc = o - mu
    var = jnp.mean(oc * oc, axis=-1, keepdims=True)
    on = oc * lax.rsqrt(var + LN_EPS) * norm_g
    return gate * jax.nn.sigmoid(gate) * on


def _ret_decay_tables(c):
    h = RET_HEADS
    lg = jnp.log1p(-jnp.exp2(-5.0 - jnp.arange(h, dtype=F32)))
    i = jnp.arange(c, dtype=F32)
    diff = i[:, None] - i[None, :]
    decay_in = jnp.where(diff[None] >= 0, jnp.exp(jnp.maximum(diff, 0.0)[None] * lg[:, None, None]), 0.0)
    q_dec = jnp.exp((i[None, :] + 1.0) * lg[:, None])
    k_dec = jnp.exp((c - 1.0 - i[None, :]) * lg[:, None])
    c_dec = jnp.exp(c * lg)
    dq = jnp.broadcast_to(q_dec[:, :, None], (h, c, RET_VAL_DIM))
    dk = jnp.broadcast_to(k_dec[:, :, None], (h, c, RET_KEY_DIM))
    dc = jnp.broadcast_to(c_dec[:, None, None], (h, 1, RET_VAL_DIM))
    return decay_in, dq, dk, dc


def _ret_core_kernel(q_ref, k_ref, v_ref, g_ref, din_ref, dq_ref, dk_ref, dc_ref, ng_ref,
                     w_ref, x_ref, lg_ref, lb_ref, y_ref, s_ref, o_ref):
    c = pl.program_id(1)

    @pl.when(c == 0)
    def _():
        s_ref[...] = jnp.zeros_like(s_ref)

    for h in range(RET_HEADS):
        ks = slice(h * RET_KEY_DIM, (h + 1) * RET_KEY_DIM)
        vs = slice(h * RET_VAL_DIM, (h + 1) * RET_VAL_DIM)
        q = q_ref[0, :, ks]
        k = k_ref[0, :, ks]
        v = v_ref[0, :, vs]
        s = s_ref[0, h]
        att = _dot_nt(q, k) * din_ref[h]
        inner = _dot(att.astype(BF16), v)
        cross = _dot(q, s.astype(BF16)) * dq_ref[h]
        kt = (k.astype(F32) * dk_ref[h]).T.astype(BF16)
        s_ref[0, h] = s * dc_ref[h] + _dot(kt, v)
        o_ref[:, vs] = _group_norm_gate(inner + cross, ng_ref[:, vs],
                                        g_ref[0, :, vs].astype(F32)).astype(BF16)
    y = _dot(o_ref[...], w_ref[...])
    y_ref[0] = _layer_norm(DEEPNORM_ALPHA * x_ref[0] + y, lg_ref[...], lb_ref[...])


def _ret_layer_prompt(q, k, v, g, norm_g, w_out, layer, x, ln_g, ln_b):
    b, l, _ = q.shape
    c = math.gcd(l, RET_BLOCK)
    hk = RET_HEADS * RET_KEY_DIM
    hv = RET_HEADS * RET_VAL_DIM
    decay_in, dq, dk, dc = _ret_decay_tables(c)
    kspec = pl.BlockSpec((1, c, hk), lambda bi, ci: (bi, ci, 0))
    vspec = pl.BlockSpec((1, c, hv), lambda bi, ci: (bi, ci, 0))
    xspec = pl.BlockSpec((1, c, D_MODEL), lambda bi, ci: (bi, ci, 0))
    return pl.pallas_call(
        _ret_core_kernel,
        grid=(b, l // c),
        in_specs=[kspec, kspec, vspec, vspec, _resident(decay_in.shape), _resident(dq.shape),
                  _resident(dk.shape), _resident(dc.shape), _resident(norm_g.shape),
                  _resident_layer(w_out, layer), xspec, _resident((1, D_MODEL)), _resident((1, D_MODEL))],
        out_specs=[xspec,
                   pl.BlockSpec((1, RET_HEADS, RET_KEY_DIM, RET_VAL_DIM), lambda bi, ci: (bi, 0, 0, 0))],
        out_shape=[jax.ShapeDtypeStruct((b, l, D_MODEL), F32),
                   jax.ShapeDtypeStruct((b, RET_HEADS, RET_KEY_DIM, RET_VAL_DIM), F32)],
        scratch_shapes=[pltpu.VMEM((c, hv), BF16)],
        compiler_params=_params("parallel", "arbitrary"),
        name="ret_layer",
    )(q, k, v, g, decay_in, dq, dk, dc, norm_g, w_out, x, ln_g, ln_b)


def _column(row, n):
    r = lax.broadcasted_iota(I32, (n, n), 0)
    c = lax.broadcasted_iota(I32, (n, n), 1)
    return jnp.sum(jnp.where(r == c, jnp.broadcast_to(row, (n, n)), 0.0), axis=1, keepdims=True)


def _ret_sample_kernel(q_ref, k_ref, v_ref, g_ref, s_ref, gam_ref, ng_ref, *rest):
    o_ref, so_ref = rest[-2:]
    for r in range(q_ref.shape[0]):
        for h in range(RET_HEADS):
            ks = slice(h * RET_KEY_DIM, (h + 1) * RET_KEY_DIM)
            vs = slice(h * RET_VAL_DIM, (h + 1) * RET_VAL_DIM)
            qcol = _column(q_ref[r, :, ks], RET_KEY_DIM)
            kcol = _column(k_ref[r, :, ks], RET_KEY_DIM)
            v = v_ref[r, :, vs]
            s_new = s_ref[0, r, h] * gam_ref[h] + kcol * v
            so_ref[0, r, h] = s_new
            o = jnp.sum(qcol * s_new, axis=0, keepdims=True)
            o_ref[r, :, vs] = _group_norm_gate(o, ng_ref[:, vs], g_ref[r, :, vs])


def _ret_core_sample(q, k, v, g, state_ret, j, norm_g, new_states):
    db = q.shape[0]
    lg = jnp.log1p(-jnp.exp2(-5.0 - jnp.arange(RET_HEADS, dtype=F32)))
    gam = jnp.broadcast_to(jnp.exp(lg)[:, None, None], (RET_HEADS, 1, RET_VAL_DIM))
    tb = 4 if db % 4 == 0 else 1
    kspec = pl.BlockSpec((tb, 1, RET_HEADS * RET_KEY_DIM), lambda bi: (bi, 0, 0))
    vspec = pl.BlockSpec((tb, 1, RET_HEADS * RET_VAL_DIM), lambda bi: (bi, 0, 0))
    sspec = pl.BlockSpec((1, tb, RET_HEADS, RET_KEY_DIM, RET_VAL_DIM), lambda bi: (j, bi, 0, 0, 0))
    in_specs = [kspec, kspec, vspec, vspec, sspec, _resident(gam.shape), _resident(norm_g.shape)]
    args = [q, k, v, g, state_ret, gam, norm_g]
    aliases = {}
    if new_states is not None:
        in_specs.append(pl.BlockSpec(memory_space=pl.ANY))
        args.append(new_states)
        aliases = {len(args) - 1: 1}
    return pl.pallas_call(
        _ret_sample_kernel,
        grid=(db // tb,),
        in_specs=in_specs,
        out_specs=[vspec, sspec],
        out_shape=[jax.ShapeDtypeStruct((db, 1, RET_HEADS * RET_VAL_DIM), F32),
                   jax.ShapeDtypeStruct(state_ret.shape, F32)],
        input_output_aliases=aliases,
        compiler_params=_params("parallel"),
        name="ret_sample",
    )(*args)


POOL_HALO = 16


def _pool_prompt_kernel(x_ref, xh_ref, w_ref, sc_ref, g_ref, b_ref, o_ref, ext_ref, z_ref):
    i = pl.program_id(1)
    tm = x_ref.shape[1]
    x = x_ref[0]
    halo = jnp.where(i == 0, 0.0, xh_ref[0])
    ext_ref[0:POOL_HALO, :] = halo
    ext_ref[POOL_HALO:POOL_HALO + tm, :] = x
    pos = i * tm + lax.broadcasted_iota(I32, (tm, 1), 0)
    for gi, w in enumerate(POOL_WINDOWS):
        cs = slice(gi * POOL_GROUP_DIM, (gi + 1) * POOL_GROUP_DIM)
        wsum = ext_ref[POOL_HALO:POOL_HALO + tm, cs]
        for jj in range(1, w):
            wsum = wsum + ext_ref[POOL_HALO - jj:POOL_HALO - jj + tm, cs]
        cnt = jnp.minimum(w, pos + 1).astype(F32)
        d = wsum / cnt - x[:, cs]
        y = _dot(d.astype(BF16), w_ref[gi]) * sc_ref[:, cs]
        z_ref[:, cs] = DEEPNORM_ALPHA * x[:, cs] + y
    o_ref[0] = _layer_norm(z_ref[...], g_ref[...], b_ref[...])


def _pool_layer_prompt(x, w_grp, scale, g, b):
    bsz, l, _ = x.shape
    tm = _row_tile(l)
    hb = tm // POOL_HALO
    return pl.pallas_call(
        _pool_prompt_kernel,
        grid=(bsz, l // tm),
        in_specs=[pl.BlockSpec((1, tm, D_MODEL), lambda bi, i: (bi, i, 0)),
                  pl.BlockSpec((1, POOL_HALO, D_MODEL), lambda bi, i: (bi, jnp.maximum(i * hb - 1, 0), 0)),
                  _resident(w_grp.shape), _resident((1, D_MODEL)), _resident((1, D_MODEL)),
                  _resident((1, D_MODEL))],
        out_specs=pl.BlockSpec((1, tm, D_MODEL), lambda bi, i: (bi, i, 0)),
        out_shape=jax.ShapeDtypeStruct((bsz, l, D_MODEL), F32),
        scratch_shapes=[pltpu.VMEM((POOL_HALO + tm, D_MODEL), F32), pltpu.VMEM((tm, D_MODEL), F32)],
        compiler_params=_params("parallel", "arbitrary"),
        name="pool_prompt",
    )(x, x, w_grp, scale, g, b)


def _pool_sample_kernel(x_ref, h_ref, w_ref, sc_ref, g_ref, b_ref, o_ref, z_ref):
    x = x_ref[...]
    for gi, w in enumerate(POOL_WINDOWS):
        cs = slice(gi * POOL_GROUP_DIM, (gi + 1) * POOL_GROUP_DIM)
        wsum = x[:, cs]
        for jj in range(1, w):
            wsum = wsum + h_ref[POOL_HIST - jj, :, cs]
        d = wsum / float(w) - x[:, cs]
        y = _dot(d.astype(BF16), w_ref[gi]) * sc_ref[:, cs]
        z_ref[:, cs] = DEEPNORM_ALPHA * x[:, cs] + y
    o_ref[...] = _layer_norm(z_ref[...], g_ref[...], b_ref[...])


def _pool_layer_sample(x, hist_t, w_grp, scale, g, b):
    db = x.shape[0]
    return pl.pallas_call(
        _pool_sample_kernel,
        grid=(1,),
        in_specs=[_resident(x.shape), _resident(hist_t.shape), _resident(w_grp.shape),
                  _resident((1, D_MODEL)), _resident((1, D_MODEL)), _resident((1, D_MODEL))],
        out_specs=pl.BlockSpec((db, D_MODEL), lambda i: (0, 0)),
        out_shape=jax.ShapeDtypeStruct((db, D_MODEL), F32),
        scratch_shapes=[pltpu.VMEM((db, D_MODEL), F32)],
        compiler_params=_params("arbitrary"),
        name="pool_sample",
    )(x, hist_t, w_grp, scale, g, b)


DSA_QKV = 3 * D_MODEL
DSA_QI_PAD = IDX_HEADS * LANES
DSA_PAD_IN = DSA_QKV + DSA_QI_PAD + LANES
WI_LANE = IDX_DIM


def _dsa_pad_w_in(w_in):
    d = w_in.shape[0]
    qi0 = DSA_QKV
    ki0 = qi0 + IDX_HEADS * IDX_DIM
    z = jnp.zeros((d, LANES - IDX_DIM), w_in.dtype)
    parts = [w_in[:, :DSA_QKV]]
    for h in range(IDX_HEADS):
        parts += [w_in[:, qi0 + h * IDX_DIM:qi0 + (h + 1) * IDX_DIM], z]
    parts += [w_in[:, ki0:], jnp.zeros((d, LANES - IDX_DIM - IDX_HEADS), w_in.dtype)]
    return jnp.concatenate(parts, axis=1)


def _rope_rolled(y, cos, sin, half):
    if 2 * half == LANES:
        rot = pltpu.roll(y, half, 1)
    else:
        lane = lax.broadcasted_iota(I32, y.shape, 1)
        rot = jnp.where(lane < half, pltpu.roll(y, LANES - half, 1), pltpu.roll(y, half, 1))
    return y * cos + rot * sin


def _dsa_proj_kernel(x_ref, w_ref, cq_ref, sq_ref, ci_ref, si_ref,
                     qb_ref, k_ref, v_ref, kb_ref, vb_ref, qi_ref, kiw_ref):
    xb = x_ref[...].astype(BF16)
    cq = cq_ref[...]
    sq = sq_ref[...]
    ci = ci_ref[...]
    si = si_ref[...]
    hd = ATTN_HEAD_DIM
    wide = 4 * hd
    for c in range(0, D_MODEL, wide):
        yq = _dot(xb, w_ref[:, c:c + wide])
        yk = _dot(xb, w_ref[:, D_MODEL + c:D_MODEL + c + wide])
        yv = _dot(xb, w_ref[:, 2 * D_MODEL + c:2 * D_MODEL + c + wide])
        v_ref[:, c:c + wide] = yv
        vb_ref[:, c:c + wide] = yv.astype(BF16)
        for s in range(0, wide, hd):
            cs = slice(c + s, c + s + hd)
            qb_ref[:, cs] = _rope_rolled(yq[:, s:s + hd], cq, sq, hd // 2).astype(BF16)
            kr = _rope_rolled(yk[:, s:s + hd], cq, sq, hd // 2)
            k_ref[:, cs] = kr
            kb_ref[:, cs] = kr.astype(BF16)
    yi = _dot(xb, w_ref[:, DSA_QKV:DSA_QKV + DSA_QI_PAD + LANES])
    for h in range(IDX_HEADS):
        qi_ref[:, h * LANES:(h + 1) * LANES] = _rope_rolled(yi[:, h * LANES:(h + 1) * LANES], ci, si, IDX_DIM // 2)
    kiw_ref[...] = _rope_rolled(yi[:, DSA_QI_PAD:], ci, si, IDX_DIM // 2)


def _dsa_proj(x, w_pad, tabs, pos_tiles):
    m = x.shape[0]
    tm = _row_tile(m)
    tab = pl.BlockSpec((tm, LANES), lambda i: (i % pos_tiles, 0))
    row = lambda n: pl.BlockSpec((tm, n), lambda i: (i, 0))
    sds = lambda n, dt: jax.ShapeDtypeStruct((m, n), dt)
    return pl.pallas_call(
        _dsa_proj_kernel,
        grid=(m // tm,),
        in_specs=[row(D_MODEL), _resident(w_pad.shape), tab, tab, tab, tab],
        out_specs=[row(D_MODEL), row(D_MODEL), row(D_MODEL), row(D_MODEL), row(D_MODEL),
                   row(DSA_QI_PAD), row(LANES)],
        out_shape=[sds(D_MODEL, BF16), sds(D_MODEL, F32), sds(D_MODEL, F32), sds(D_MODEL, BF16),
                   sds(D_MODEL, BF16), sds(DSA_QI_PAD, F32), sds(LANES, F32)],
        compiler_params=_params("parallel"),
        name="dsa_proj",
    )(x, w_pad, *tabs)


F32_ROWS = 8
I16_ROWS = 16


def _topk_bias_t(score, kidx, limit, n_top, idx_bits, sc_ref, eq_ref):
    n, q = score.shape
    vis = kidx <= jnp.broadcast_to(jnp.asarray(limit, I32), (n, q))
    sc_ref[0:n, :] = jnp.where(vis, score, NEG_INF)

    def count(pick, slabs):
        part = pick(0)
        for s in range(1, slabs):
            part = part + pick(s)
        return jnp.sum(part.astype(I32), axis=0, keepdims=True)

    def as_float(key):
        key = jnp.maximum(key, INT32_MIN + 0x7FFFFF)
        return lax.bitcast_convert_type(key ^ ((key >> 31) & 0x7FFFFFFF), F32)

    def count_ge(key):
        c = jnp.broadcast_to(as_float(key), (F32_ROWS, q))
        return count(lambda s: jnp.where(sc_ref[s * F32_ROWS:(s + 1) * F32_ROWS, :] >= c, 1, 0),
                     n // F32_ROWS)

    z = jnp.zeros((1, q), I32)
    t = jnp.where(count_ge(z) >= n_top, z, INT32_MIN)

    def value_bit(i, t):
        cand = t + jnp.left_shift(jnp.int32(1), 30 - i)
        return jnp.where(count_ge(cand) >= n_top, cand, t)

    t = jnp.broadcast_to(as_float(lax.fori_loop(0, 31, value_bit, t)), (n, q))
    sc = sc_ref[0:n, :]
    gt = sc > t
    eq = sc == t
    gt_i = jnp.where(gt, 1, 0)
    need = n_top - count(lambda s: gt_i[s * F32_ROWS:(s + 1) * F32_ROWS, :], n // F32_ROWS)
    eq_ref[0:n, :] = jnp.where(eq, 1, 0).astype(I16)
    zero = jnp.zeros((), I16)
    sub = lax.broadcasted_iota(I16, (I16_ROWS, q), 0)

    def index_bit(i, j):
        cand = j + jnp.left_shift(jnp.int32(1), idx_bits - 1 - i)
        c16 = jnp.broadcast_to(cand.astype(I16), (I16_ROWS, q))
        before = count(lambda s: jnp.where(sub + jnp.int16(s * I16_ROWS) < c16,
                                           eq_ref[s * I16_ROWS:(s + 1) * I16_ROWS, :], zero), n // I16_ROWS)
        return jnp.where(before < need, cand, j)

    j = lax.fori_loop(0, idx_bits, index_bit, z)
    picked = jnp.where(gt, 0.0, jnp.where(eq, jnp.where(kidx <= jnp.broadcast_to(j, (n, q)), 0.0, NEG_INF), NEG_INF))
    return jnp.where(vis, picked, NEG_INF)


def _stack_idx_heads(qi):
    return jnp.concatenate([qi[:, h * LANES:h * LANES + IDX_DIM] for h in range(IDX_HEADS)],
                           axis=0).astype(BF16)


def _dsa_attn_prompt_kernel(q_ref, qi_ref, wt_ref, kiw_ref, kb_ref, vb_ref, w_ref, x_ref, lg_ref, lb_ref,
                            y_ref, sc_ref, eq_ref, o_ref, *, n_top, widths):
    i = pl.program_id(1)
    tq = q_ref.shape[1]
    hd = ATTN_HEAD_DIM

    def attend(n):
        r = _dot_nt(kiw_ref[0, :n, :IDX_DIM].astype(BF16), _stack_idx_heads(qi_ref[0]))
        score = jnp.zeros((n, tq), F32)
        for h in range(IDX_HEADS):
            score = score + (wt_ref[0, h:h + 1, :] * IDX_SCALE) * jnp.maximum(r[:, h * tq:(h + 1) * tq], 0.0)
        kidx = lax.broadcasted_iota(I32, (n, tq), 0)
        pos = i * tq + lax.broadcasted_iota(I32, (1, tq), 1)
        bias = _topk_bias_t(score, kidx, pos, n_top, max(1, (n - 1).bit_length()),
                            sc_ref, eq_ref).T
        for h in range(ATTN_HEADS):
            cs = slice(h * hd, (h + 1) * hd)
            s = _dot_nt(q_ref[0, :, cs], kb_ref[0, :n, cs]) * (hd ** -0.5) + bias
            m = jnp.max(s, axis=-1, keepdims=True)
            p = jnp.exp(s - m)
            l = jnp.sum(p, axis=-1, keepdims=True)
            o_ref[:, cs] = (_dot(p.astype(BF16), vb_ref[0, :n, cs]) / l).astype(BF16)

    last = (i + 1) * tq
    prev = 0
    for n in widths:
        pl.when(jnp.logical_and(last > prev, last <= n))(functools.partial(attend, n))
        prev = n
    y = _dot(o_ref[...], w_ref[...])
    y_ref[0] = _layer_norm(DEEPNORM_ALPHA * x_ref[0] + y, lg_ref[...], lb_ref[...])


def _dsa_layer_prompt(qb, qi, kiw, kb, vb, n_top, w_out, layer, x, ln_g, ln_b):
    b, l, _ = qb.shape
    tq = 256 if l % 256 == 0 else 128
    step = 2 * tq if l % (2 * tq) == 0 else tq
    widths = tuple(range(step, l + 1, step))
    wt = jnp.swapaxes(kiw[:, :, WI_LANE:WI_LANE + IDX_HEADS], 1, 2)
    qrow = lambda n: pl.BlockSpec((1, tq, n), lambda bi, i: (bi, i, 0))
    full = lambda n: pl.BlockSpec((1, l, n), lambda bi, i: (bi, 0, 0))
    return pl.pallas_call(
        functools.partial(_dsa_attn_prompt_kernel, n_top=n_top, widths=widths),
        grid=(b, l // tq),
        in_specs=[qrow(D_MODEL), qrow(DSA_QI_PAD), pl.BlockSpec((1, IDX_HEADS, tq), lambda bi, i: (bi, 0, i)),
                  full(LANES), full(D_MODEL), full(D_MODEL),
                  _resident_layer(w_out, layer), qrow(D_MODEL), _resident((1, D_MODEL)), _resident((1, D_MODEL))],
        out_specs=qrow(D_MODEL),
        out_shape=jax.ShapeDtypeStruct((b, l, D_MODEL), F32),
        scratch_shapes=[pltpu.VMEM((l, tq), F32), pltpu.VMEM((l, tq), I16), pltpu.VMEM((tq, D_MODEL), BF16)],
        compiler_params=_params("parallel", "arbitrary"),
        name="dsa_layer_prompt",
    )(qb, qi, wt, kiw, kb, vb, w_out, x, ln_g, ln_b)


def _dsa_idx_sample_kernel(pt_ref, qi_ref, kiwq_ref, *rest):
    del pt_ref
    pages, o_ref = rest[:-1], rest[-1]
    qs = _stack_idx_heads(qi_ref[0])
    wq = kiwq_ref[0]
    rows = wq.shape[0]
    for p, page_ref in enumerate(pages):
        r = _dot(qs, page_ref[0].astype(BF16))
        acc = jnp.zeros((rows, PAGE_SIZE), F32)
        for h in range(IDX_HEADS):
            acc = acc + (wq[:, WI_LANE + h:WI_LANE + h + 1] * IDX_SCALE) * jnp.maximum(
                r[h * rows:(h + 1) * rows], 0.0)
        o_ref[0, :, p * PAGE_SIZE:(p + 1) * PAGE_SIZE] = acc


def _dsa_idx_sample(page_table, qi, kiw, cache_kidx_t):
    db, n_pages = page_table.shape
    qi8 = jnp.broadcast_to(qi[:, None, :], (db, 8, DSA_QI_PAD))
    kiw8 = jnp.broadcast_to(kiw[:, None, :], (db, 8, LANES))
    page_specs = [pl.BlockSpec((1, IDX_DIM, PAGE_SIZE), functools.partial(lambda p, bi, pt: (pt[bi, p], 0, 0), p))
                  for p in range(n_pages)]
    out = pl.pallas_call(
        _dsa_idx_sample_kernel,
        grid_spec=pltpu.PrefetchScalarGridSpec(
            num_scalar_prefetch=1,
            grid=(db,),
            in_specs=[pl.BlockSpec((1, 8, DSA_QI_PAD), lambda bi, pt: (bi, 0, 0)),
                      pl.BlockSpec((1, 8, LANES), lambda bi, pt: (bi, 0, 0))] + page_specs,
            out_specs=pl.BlockSpec((1, 8, n_pages * PAGE_SIZE), lambda bi, pt: (bi, 0, 0)),
        ),
        out_shape=jax.ShapeDtypeStruct((db, 8, n_pages * PAGE_SIZE), F32),
        compiler_params=_params("arbitrary"),
        name="dsa_idx_sample",
    )(page_table, qi8, kiw8, *([cache_kidx_t] * n_pages))
    return out[:, 0, :]


def _dsa_sel_sample_kernel(spt_ref, qi_ref, kiw_ref, o_ref, sc_ref, eq_ref, *, past, n_top, idx_bits):
    db = qi_ref.shape[0]
    kiw = kiw_ref[...]
    lane = lax.broadcasted_iota(I32, kiw.shape, 1)
    ki = jnp.where(lane < IDX_DIM, kiw, 0.0)
    qi = qi_ref[...]
    self_score = jnp.zeros((db, 1), F32)
    for h in range(IDX_HEADS):
        r = jnp.sum(qi[:, h * LANES:(h + 1) * LANES] * ki, axis=-1, keepdims=True)
        self_score = self_score + (kiw[:, WI_LANE + h:WI_LANE + h + 1] * IDX_SCALE) * jnp.maximum(r, 0.0)
    self_rows = jnp.broadcast_to(self_score, (db, LANES)).T[0:I16_ROWS, :]
    n = past + I16_ROWS
    score = jnp.concatenate([spt_ref[...], self_rows], axis=0)
    kidx = lax.broadcasted_iota(I32, (n, db), 0)
    o_ref[...] = _topk_bias_t(score, kidx, past, n_top, idx_bits, sc_ref, eq_ref)


def _dsa_sel_sample(score_past, qi, kiw, n_top):
    db, past = score_past.shape
    n = past + I16_ROWS
    spt = score_past.T
    out = pl.pallas_call(
        functools.partial(_dsa_sel_sample_kernel, past=past, n_top=n_top, idx_bits=n.bit_length()),
        grid=(1,),
        in_specs=[_resident(spt.shape), _resident(qi.shape), _resident(kiw.shape)],
        out_specs=pl.BlockSpec((n, db), lambda i: (0, 0)),
        out_shape=jax.ShapeDtypeStruct((n, db), F32),
        scratch_shapes=[pltpu.VMEM((n, db), F32), pltpu.VMEM((n, db), I16)],
        compiler_params=_params("arbitrary"),
        name="dsa_sel_sample",
    )(spt, qi, kiw)
    return out.T


def _dsa_attn_sample_kernel(pt_ref, q_ref, kn_ref, vn_ref, bias_ref, bself_ref, *rest, n_pages):
    del pt_ref
    kpages, vpages, o_ref = rest[:n_pages], rest[n_pages:2 * n_pages], rest[-1]
    hd = ATTN_HEAD_DIM
    rows = PAGE_SIZE * ATTN_HEADS
    qb = q_ref[0]
    scale = hd ** -0.5
    head = lax.broadcasted_iota(I32, (ATTN_HEADS, rows), 0)
    col = lax.broadcasted_iota(I32, (ATTN_HEADS, rows), 1)
    own = jnp.bitwise_and(col, ATTN_HEADS - 1) == head
    s_pages = []
    for p, kp in enumerate(kpages):
        k2 = kp[0].reshape(rows, hd).astype(BF16)
        s = _dot_nt(qb, k2) * scale + bias_ref[0, :, p * rows:(p + 1) * rows]
        s_pages.append(jnp.where(own, s, NEG_INF))
    s_self = jnp.sum(qb.astype(F32) * kn_ref[0].astype(BF16).astype(F32), axis=-1, keepdims=True) * scale
    s_self = s_self + bself_ref[0, :, 0:1]
    m = s_self
    for s in s_pages:
        m = jnp.maximum(m, jnp.max(s, axis=-1, keepdims=True))
    p_self = jnp.exp(s_self - m)
    l = p_self
    acc = p_self.astype(BF16).astype(F32) * vn_ref[0].astype(BF16).astype(F32)
    for s, vp in zip(s_pages, vpages):
        pexp = jnp.exp(s - m)
        l = l + jnp.sum(pexp, axis=-1, keepdims=True)
        acc = acc + _dot(pexp.astype(BF16), vp[0].reshape(rows, hd).astype(BF16))
    o_ref[0] = acc / l


def _dsa_attn_sample(page_table, q, k_new, v_new, bias, cache_k, cache_v):
    db, n_pages = page_table.shape
    past = n_pages * PAGE_SIZE
    bias_keys = jnp.repeat(bias[:, :past], ATTN_HEADS, axis=1)[:, None, :]
    bias_self = bias[:, None, past:]
    row = pl.BlockSpec((1, ATTN_HEADS, ATTN_HEAD_DIM), lambda bi, pt: (bi, 0, 0))
    page_spec = lambda p: pl.BlockSpec((1, PAGE_SIZE, ATTN_HEADS, ATTN_HEAD_DIM),
                                       functools.partial(lambda p, bi, pt: (pt[bi, p], 0, 0, 0), p))
    return pl.pallas_call(
        functools.partial(_dsa_attn_sample_kernel, n_pages=n_pages),
        grid_spec=pltpu.PrefetchScalarGridSpec(
            num_scalar_prefetch=1,
            grid=(db,),
            in_specs=[row, row, row,
                      pl.BlockSpec((1, 1, past * ATTN_HEADS), lambda bi, pt: (bi, 0, 0)),
                      pl.BlockSpec((1, 1, bias_self.shape[-1]), lambda bi, pt: (bi, 0, 0))]
                     + [page_spec(p) for p in range(n_pages)] + [page_spec(p) for p in range(n_pages)],
            out_specs=row,
        ),
        out_shape=jax.ShapeDtypeStruct((db, ATTN_HEADS, ATTN_HEAD_DIM), F32),
        compiler_params=_params("arbitrary"),
        name="dsa_attn_sample",
    )(page_table, q, k_new, v_new, bias_keys, bias_self, *([cache_k] * n_pages), *([cache_v] * n_pages))


def kernel(x_prompt, x_sample, state_ret, state_pool, cache_k, cache_v, cache_kidx, page_table,
           ret_w_in, ret_norm_g, ret_w_out, pool_w_grp, pool_scale, dsa_w_in, dsa_w_out,
           ffn_w_in, ffn_w_out, ln_g, ln_b):
    bp, lp, d = x_prompt.shape
    db, ls, _ = x_sample.shape
    assert d == D_MODEL and ls == 1
    past = page_table.shape[1] * PAGE_SIZE
    mp = bp * lp
    tmp = _row_tile(mp)
    assert lp % tmp == 0
    xp = x_prompt.reshape(mp, d)
    xs = x_sample.reshape(db, d)
    pos_p = jnp.arange(lp)
    pos_s = jnp.full((db,), past, jnp.int32)

    ret_w_in_b, ret_w_out_b = ret_w_in.astype(BF16), ret_w_out.astype(BF16)
    ffn_w_in_b, ffn_w_out_b = ffn_w_in.astype(BF16), ffn_w_out.astype(BF16)
    dsa_w_out_b = dsa_w_out.astype(BF16)
    ret_p, pool_p, pool_s = [], [], []
    ret_s = None
    kp_l, vp_l, kip_l, ks_l, vs_l, kis_l = [], [], [], [], [], []
    for i in range(DEPTH):
        kind, j = i % 3, i // 3
        g1, b1 = ln_g[i, 0][None, :], ln_b[i, 0][None, :]
        g2, b2 = ln_g[i, 1][None, :], ln_b[i, 1][None, :]
        if kind == 0:
            ng = ret_norm_g[j][None, :]
            half = RET_KEY_DIM // 2
            cos_p, sin_p = _rope_tables_split(pos_p, half)
            cos_s, sin_s = _rope_tables_split(pos_s, half)
            q, k, v, g = _ret_proj(xp, ret_w_in_b, j, cos_p, sin_p, lp // tmp, BF16)
            shp = lambda a: a.reshape(bp, lp, a.shape[-1])
            xp, sp = _ret_layer_prompt(shp(q), shp(k), shp(v), shp(g), ng, ret_w_out_b, j, shp(xp), g1, b1)
            xp = xp.reshape(mp, d)
            q, k, v, g = _ret_proj(xs, ret_w_in_b, j, cos_s, sin_s, 1, F32)
            o, ret_s = _ret_core_sample(q[:, None, :], k[:, None, :], v[:, None, :], g[:, None, :],
                                        state_ret, j, ng, ret_s)
            xs = _out_ln(o.reshape(db, -1), ret_w_out_b, j, xs, g1, b1)
            ret_p.append(sp)
        elif kind == 1:
            w_grp = pool_w_grp[j].astype(BF16)
            sc = pool_scale[j][None, :]
            pool_p.append(xp.reshape(bp, lp, d)[:, -POOL_HIST:])
            pool_s.append(jnp.concatenate([state_pool[j][:, 1:], xs[:, None, :]], axis=1))
            xp = _pool_layer_prompt(xp.reshape(bp, lp, d), w_grp, sc, g1, b1).reshape(mp, d)
            xs = _pool_layer_sample(xs, jnp.swapaxes(state_pool[j], 0, 1), w_grp, sc, g1, b1)
        else:
            w_pad = _dsa_pad_w_in(dsa_w_in[j]).astype(BF16)
            tabs_p = _rope_tables_rolled(pos_p, ATTN_HEAD_DIM // 2) + _rope_tables_rolled(pos_p, IDX_DIM // 2)
            tabs_s = _rope_tables_rolled(pos_s, ATTN_HEAD_DIM // 2) + _rope_tables_rolled(pos_s, IDX_DIM // 2)
            qb, k32, v32, kb, vb, qi, kiw = _dsa_proj(xp, w_pad, tabs_p, lp // tmp)
            shp = lambda a: a.reshape(bp, lp, a.shape[-1])
            xp = _dsa_layer_prompt(shp(qb), shp(qi), shp(kiw), shp(kb), shp(vb), min(TOPK_MAX, lp // 4),
                                   dsa_w_out_b, j, shp(xp), g1, b1).reshape(mp, d)
            kp_l.append(k32.reshape(bp, lp, ATTN_HEADS, ATTN_HEAD_DIM))
            vp_l.append(v32.reshape(bp, lp, ATTN_HEADS, ATTN_HEAD_DIM))
            kip_l.append(kiw[:, :IDX_DIM].reshape(bp, lp, IDX_DIM))

            qb, k32, v32, kb, vb, qi, kiw = _dsa_proj(xs, w_pad, tabs_s, 1)
            score_past = _dsa_idx_sample(page_table, qi, kiw, jnp.swapaxes(cache_kidx[j], -1, -2))
            bias = _dsa_sel_sample(score_past, qi, kiw, min(TOPK_MAX, (past + ls) // 4))
            heads = lambda a: a.reshape(db, ATTN_HEADS, ATTN_HEAD_DIM)
            o = _dsa_attn_sample(page_table, heads(qb), heads(k32), heads(v32), bias, cache_k[j], cache_v[j])
            xs = _out_ln(o.reshape(db, d), dsa_w_out_b, j, xs, g1, b1)
            ks_l.append(k32.reshape(db, ls, ATTN_HEADS, ATTN_HEAD_DIM))
            vs_l.append(v32.reshape(db, ls, ATTN_HEADS, ATTN_HEAD_DIM))
            kis_l.append(kiw[:, :IDX_DIM].reshape(db, ls, IDX_DIM))
        xp = _ffn_ln(xp, ffn_w_in_b, ffn_w_out_b, i, g2, b2)
        xs = _ffn_ln(xs, ffn_w_in_b, ffn_w_out_b, i, g2, b2)
    return (xp.reshape(bp, lp, d), xs.reshape(db, ls, d), jnp.stack(ret_p), ret_s,
            jnp.stack(pool_p), jnp.stack(pool_s), jnp.stack(kp_l), jnp.stack(vp_l), jnp.stack(kip_l),
            jnp.stack(ks_l), jnp.stack(vs_l), jnp.stack(kis_l))
```

```python
import functools
import math

import jax
import jax.numpy as jnp
from jax import lax
from jax.experimental import pallas as pl
from jax.experimental.pallas import tpu as pltpu

F32 = jnp.float32
BF16 = jnp.bfloat16
I32 = jnp.int32
I16 = jnp.int16

LANES = 128
VMEM_LIMIT_BYTES = 56 * 1024 * 1024

D_MODEL = 1024
DEPTH = 4
PAGE_SIZE = 128
RET_HEADS = 4
RET_KEY_DIM = D_MODEL // RET_HEADS
RET_VAL_DIM = 2 * RET_KEY_DIM
RET_BLOCK = 256
POOL_WINDOWS = (2, 4, 8, 16)
POOL_GROUP_DIM = D_MODEL // len(POOL_WINDOWS)
POOL_HIST = max(POOL_WINDOWS) - 1
ATTN_HEADS = 8
ATTN_HEAD_DIM = D_MODEL // ATTN_HEADS
IDX_HEADS = 4
IDX_DIM = 64
IDX_SCALE = (IDX_HEADS ** -0.5) * (IDX_DIM ** -0.5)
TOPK_MAX = 256
FFN_DIM = -(-8 * D_MODEL // (3 * 256)) * 256
ROPE_THETA = 10000.0
LN_EPS = 1e-5
DEEPNORM_ALPHA = (2.0 * DEPTH) ** 0.25
INT32_MIN = -(2 ** 31)
NEG_INF = float("-inf")


def _params(*sem):
    return pltpu.CompilerParams(dimension_semantics=sem, vmem_limit_bytes=VMEM_LIMIT_BYTES)


def _resident(shape):
    nd = len(shape)
    return pl.BlockSpec(shape, lambda *_: (0,) * nd, pipeline_mode=pl.Buffered(1))


def _resident_layer(stacked, layer):
    shape = stacked.shape[1:]
    nd = len(shape)
    return pl.BlockSpec((None,) + shape, lambda *_: (layer,) + (0,) * nd, pipeline_mode=pl.Buffered(1))


def _row_tile(m):
    for t in (512, 256, 128):
        if m % t == 0:
            return t
    raise ValueError(f"token count {m} is not a multiple of 128")


def _layer_norm(z, g, b):
    mu = jnp.mean(z, axis=-1, keepdims=True)
    zc = z - mu
    var = jnp.mean(zc * zc, axis=-1, keepdims=True)
    return zc * lax.rsqrt(var + LN_EPS) * g + b


def _dot(a, b):
    return jnp.dot(a, b, preferred_element_type=F32)


def _dot_nt(a, b, precision=None):
    return lax.dot_general(a, b, (((1,), (1,)), ((), ())), precision=precision,
                           preferred_element_type=F32)


def _rope_angles(pos, half):
    inv = jnp.power(ROPE_THETA, -jnp.arange(half, dtype=F32) / half)
    return pos.astype(F32)[:, None] * inv[None, :]


def _rope_tables_split(pos, half):
    ang = _rope_angles(pos, half)
    return jnp.cos(ang), jnp.sin(ang)


def _rope_tables_rolled(pos, half, width=LANES):
    ang = _rope_angles(pos, half)
    cos = jnp.concatenate([jnp.cos(ang), jnp.cos(ang)], axis=-1)
    sin = jnp.concatenate([-jnp.sin(ang), jnp.sin(ang)], axis=-1)
    pad = width - 2 * half
    if pad:
        cos = jnp.concatenate([cos, jnp.ones((pos.shape[0], pad), F32)], axis=-1)
        sin = jnp.concatenate([sin, jnp.zeros((pos.shape[0], pad), F32)], axis=-1)
    return cos, sin


def _ret_proj_kernel(x_ref, w_ref, cos_ref, sin_ref, q_ref, k_ref, v_ref, g_ref):
    xb = x_ref[...].astype(BF16)
    cos = cos_ref[...]
    sin = sin_ref[...]
    hk = RET_HEADS * RET_KEY_DIM
    hv = RET_HEADS * RET_VAL_DIM
    half = RET_KEY_DIM // 2
    for h in range(RET_HEADS):
        for base, o_ref, scale in ((0, q_ref, None), (hk, k_ref, RET_KEY_DIM ** -0.5)):
            c0 = base + h * RET_KEY_DIM
            y = _dot(xb, w_ref[:, c0:c0 + RET_KEY_DIM])
            x1 = y[:, :half]
            x2 = y[:, half:]
            o1 = x1 * cos - x2 * sin
            o2 = x2 * cos + x1 * sin
            if scale is not None:
                o1 = o1 * scale
                o2 = o2 * scale
            o_ref[:, h * RET_KEY_DIM:h * RET_KEY_DIM + half] = o1.astype(o_ref.dtype)
            o_ref[:, h * RET_KEY_DIM + half:(h + 1) * RET_KEY_DIM] = o2.astype(o_ref.dtype)
    for h in range(RET_HEADS):
        c0 = h * RET_VAL_DIM
        v_ref[:, c0:c0 + RET_VAL_DIM] = _dot(
            xb, w_ref[:, 2 * hk + c0:2 * hk + c0 + RET_VAL_DIM]).astype(v_ref.dtype)
        g_ref[:, c0:c0 + RET_VAL_DIM] = _dot(
            xb, w_ref[:, 2 * hk + hv + c0:2 * hk + hv + c0 + RET_VAL_DIM]).astype(g_ref.dtype)


def _ret_proj(x, w, layer, cos, sin, pos_tiles, out_dtype):
    m = x.shape[0]
    tm = _row_tile(m)
    hk = RET_HEADS * RET_KEY_DIM
    hv = RET_HEADS * RET_VAL_DIM
    half = RET_KEY_DIM // 2
    tab = pl.BlockSpec((tm, half), lambda i: (i % pos_tiles, 0))
    return pl.pallas_call(
        _ret_proj_kernel,
        grid=(m // tm,),
        in_specs=[pl.BlockSpec((tm, D_MODEL), lambda i: (i, 0)), _resident_layer(w, layer), tab, tab],
        out_specs=[pl.BlockSpec((tm, hk), lambda i: (i, 0)), pl.BlockSpec((tm, hk), lambda i: (i, 0)),
                   pl.BlockSpec((tm, hv), lambda i: (i, 0)), pl.BlockSpec((tm, hv), lambda i: (i, 0))],
        out_shape=[jax.ShapeDtypeStruct((m, hk), out_dtype), jax.ShapeDtypeStruct((m, hk), out_dtype),
                   jax.ShapeDtypeStruct((m, hv), out_dtype), jax.ShapeDtypeStruct((m, hv), out_dtype)],
        compiler_params=_params("parallel"),
        name="ret_proj",
    )(x, w, cos, sin)


def _out_ln_kernel(a_ref, w_ref, x_ref, g_ref, b_ref, o_ref):
    y = _dot(a_ref[...].astype(BF16), w_ref[...])
    o_ref[...] = _layer_norm(DEEPNORM_ALPHA * x_ref[...] + y, g_ref[...], b_ref[...])


def _out_ln(a, w, layer, x, g, b):
    m, ka = a.shape
    tm = _row_tile(m)
    return pl.pallas_call(
        _out_ln_kernel,
        grid=(m // tm,),
        in_specs=[pl.BlockSpec((tm, ka), lambda i: (i, 0)), _resident_layer(w, layer),
                  pl.BlockSpec((tm, D_MODEL), lambda i: (i, 0)),
                  _resident((1, D_MODEL)), _resident((1, D_MODEL))],
        out_specs=pl.BlockSpec((tm, D_MODEL), lambda i: (i, 0)),
        out_shape=jax.ShapeDtypeStruct((m, D_MODEL), F32),
        compiler_params=_params("parallel"),
        name="out_ln",
    )(a, w, x, g, b)


FFN_CHUNK = 256


def _ffn_kernel(x_ref, w_in_ref, w_out_ref, g_ref, b_ref, o_ref, acc_ref):
    x = x_ref[...]
    xb = x.astype(BF16)
    for n, c in enumerate(range(0, FFN_DIM, FFN_CHUNK)):
        a = _dot(xb, w_in_ref[:, c:c + FFN_CHUNK])
        u = _dot(xb, w_in_ref[:, FFN_DIM + c:FFN_DIM + c + FFN_CHUNK])
        hact = (a * jax.nn.sigmoid(a) * u).astype(BF16)
        y = _dot(hact, w_out_ref[c:c + FFN_CHUNK, :])
        if n == 0:
            acc_ref[...] = y
        else:
            acc_ref[...] += y
    o_ref[...] = _layer_norm(DEEPNORM_ALPHA * x + acc_ref[...], g_ref[...], b_ref[...])


def _ffn_ln(x, w_in, w_out, layer, g, b):
    m = x.shape[0]
    tm = _row_tile(m)
    return pl.pallas_call(
        _ffn_kernel,
        grid=(m // tm,),
        in_specs=[pl.BlockSpec((tm, D_MODEL), lambda i: (i, 0)), _resident_layer(w_in, layer),
                  _resident_layer(w_out, layer), _resident((1, D_MODEL)), _resident((1, D_MODEL))],
        out_specs=pl.BlockSpec((tm, D_MODEL), lambda i: (i, 0)),
        out_shape=jax.ShapeDtypeStruct((m, D_MODEL), F32),
        scratch_shapes=[pltpu.VMEM((tm, D_MODEL), F32)],
        compiler_params=_params("parallel"),
        name="ffn_ln",
    )(x, w_in, w_out, g, b)


def _group_norm_gate(o, norm_g, gate):
    mu = jnp.mean(o, axis=-1, keepdims=True)
    oc = o - mu
    var = jnp.mean(oc * oc, axis=-1, keepdims=True)
    on = oc * lax.rsqrt(var + LN_EPS) * norm_g
    return gate * jax.nn.sigmoid(gate) * on


def _ret_decay_tables(c):
    h = RET_HEADS
    lg = jnp.log1p(-jnp.exp2(-5.0 - jnp.arange(h, dtype=F32)))
    i = jnp.arange(c, dtype=F32)
    diff = i[:, None] - i[None, :]
    decay_in = jnp.where(diff[None] >= 0, jnp.exp(jnp.maximum(diff, 0.0)[None] * lg[:, None, None]), 0.0)
    q_dec = jnp.exp((i[None, :] + 1.0) * lg[:, None])
    k_dec = jnp.exp((c - 1.0 - i[None, :]) * lg[:, None])
    c_dec = jnp.exp(c * lg)
    dq = jnp.broadcast_to(q_dec[:, :, None], (h, c, RET_VAL_DIM))
    dk = jnp.broadcast_to(k_dec[:, :, None], (h, c, RET_KEY_DIM))
    dc = jnp.broadcast_to(c_dec[:, None, None], (h, 1, RET_VAL_DIM))
    return decay_in, dq, dk, dc


def _ret_core_kernel(q_ref, k_ref, v_ref, g_ref, din_ref, dq_ref, dk_ref, dc_ref, ng_ref,
                     w_ref, x_ref, lg_ref, lb_ref, y_ref, s_ref, o_ref):
    c = pl.program_id(1)

    @pl.when(c == 0)
    def _():
        s_ref[...] = jnp.zeros_like(s_ref)

    for h in range(RET_HEADS):
        ks = slice(h * RET_KEY_DIM, (h + 1) * RET_KEY_DIM)
        vs = slice(h * RET_VAL_DIM, (h + 1) * RET_VAL_DIM)
        q = q_ref[0, :, ks]
        k = k_ref[0, :, ks]
        v = v_ref[0, :, vs]
        s = s_ref[0, h]
        att = _dot_nt(q, k) * din_ref[h]
        inner = _dot(att.astype(BF16), v)
        cross = _dot(q, s.astype(BF16)) * dq_ref[h]
        kt = (k.astype(F32) * dk_ref[h]).T.astype(BF16)
        s_ref[0, h] = s * dc_ref[h] + _dot(kt, v)
        o_ref[:, vs] = _group_norm_gate(inner + cross, ng_ref[:, vs],
                                        g_ref[0, :, vs].astype(F32)).astype(BF16)
    y = _dot(o_ref[...], w_ref[...])
    y_ref[0] = _layer_norm(DEEPNORM_ALPHA * x_ref[0] + y, lg_ref[...], lb_ref[...])


def _ret_layer_prompt(q, k, v, g, norm_g, w_out, layer, x, ln_g, ln_b):
    b, l, _ = q.shape
    c = math.gcd(l, RET_BLOCK)
    hk = RET_HEADS * RET_KEY_DIM
    hv = RET_HEADS * RET_VAL_DIM
    decay_in, dq, dk, dc = _ret_decay_tables(c)
    kspec = pl.BlockSpec((1, c, hk), lambda bi, ci: (bi, ci, 0))
    vspec = pl.BlockSpec((1, c, hv), lambda bi, ci: (bi, ci, 0))
    xspec = pl.BlockSpec((1, c, D_MODEL), lambda bi, ci: (bi, ci, 0))
    return pl.pallas_call(
        _ret_core_kernel,
        grid=(b, l // c),
        in_specs=[kspec, kspec, vspec, vspec, _resident(decay_in.shape), _resident(dq.shape),
                  _resident(dk.shape), _resident(dc.shape), _resident(norm_g.shape),
                  _resident_layer(w_out, layer), xspec, _resident((1, D_MODEL)), _resident((1, D_MODEL))],
        out_specs=[xspec,
                   pl.BlockSpec((1, RET_HEADS, RET_KEY_DIM, RET_VAL_DIM), lambda bi, ci: (bi, 0, 0, 0))],
        out_shape=[jax.ShapeDtypeStruct((b, l, D_MODEL), F32),
                   jax.ShapeDtypeStruct((b, RET_HEADS, RET_KEY_DIM, RET_VAL_DIM), F32)],
        scratch_shapes=[pltpu.VMEM((c, hv), BF16)],
        compiler_params=_params("parallel", "arbitrary"),
        name="ret_layer",
    )(q, k, v, g, decay_in, dq, dk, dc, norm_g, w_out, x, ln_g, ln_b)


def _column(row, n):
    r = lax.broadcasted_iota(I32, (n, n), 0)
    c = lax.broadcasted_iota(I32, (n, n), 1)
    return jnp.sum(jnp.where(r == c, jnp.broadcast_to(row, (n, n)), 0.0), axis=1, keepdims=True)


def _ret_sample_kernel(q_ref, k_ref, v_ref, g_ref, s_ref, gam_ref, ng_ref, *rest):
    o_ref, so_ref = rest[-2:]
    for r in range(q_ref.shape[0]):
        for h in range(RET_HEADS):
            ks = slice(h * RET_KEY_DIM, (h + 1) * RET_KEY_DIM)
            vs = slice(h * RET_VAL_DIM, (h + 1) * RET_VAL_DIM)
            qcol = _column(q_ref[r, :, ks], RET_KEY_DIM)
            kcol = _column(k_ref[r, :, ks], RET_KEY_DIM)
            v = v_ref[r, :, vs]
            s_new = s_ref[0, r, h] * gam_ref[h] + kcol * v
            so_ref[0, r, h] = s_new
            o = jnp.sum(qcol * s_new, axis=0, keepdims=True)
            o_ref[r, :, vs] = _group_norm_gate(o, ng_ref[:, vs], g_ref[r, :, vs])


def _ret_core_sample(q, k, v, g, state_ret, j, norm_g, new_states):
    db = q.shape[0]
    lg = jnp.log1p(-jnp.exp2(-5.0 - jnp.arange(RET_HEADS, dtype=F32)))
    gam = jnp.broadcast_to(jnp.exp(lg)[:, None, None], (RET_HEADS, 1, RET_VAL_DIM))
    tb = 4 if db % 4 == 0 else 1
    kspec = pl.BlockSpec((tb, 1, RET_HEADS * RET_KEY_DIM), lambda bi: (bi, 0, 0))
    vspec = pl.BlockSpec((tb, 1, RET_HEADS * RET_VAL_DIM), lambda bi: (bi, 0, 0))
    sspec = pl.BlockSpec((1, tb, RET_HEADS, RET_KEY_DIM, RET_VAL_DIM), lambda bi: (j, bi, 0, 0, 0))
    in_specs = [kspec, kspec, vspec, vspec, sspec, _resident(gam.shape), _resident(norm_g.shape)]
    args = [q, k, v, g, state_ret, gam, norm_g]
    aliases = {}
    if new_states is not None:
        in_specs.append(pl.BlockSpec(memory_space=pl.ANY))
        args.append(new_states)
        aliases = {len(args) - 1: 1}
    return pl.pallas_call(
        _ret_sample_kernel,
        grid=(db // tb,),
        in_specs=in_specs,
        out_specs=[vspec, sspec],
        out_shape=[jax.ShapeDtypeStruct((db, 1, RET_HEADS * RET_VAL_DIM), F32),
                   jax.ShapeDtypeStruct(state_ret.shape, F32)],
        input_output_aliases=aliases,
        compiler_params=_params("parallel"),
        name="ret_sample",
    )(*args)


POOL_HALO = 16


def _pool_prompt_kernel(x_ref, xh_ref, w_ref, sc_ref, g_ref, b_ref, o_ref, ext_ref, z_ref):
    i = pl.program_id(1)
    tm = x_ref.shape[1]
    x = x_ref[0]
    halo = jnp.where(i == 0, 0.0, xh_ref[0])
    ext_ref[0:POOL_HALO, :] = halo
    ext_ref[POOL_HALO:POOL_HALO + tm, :] = x
    pos = i * tm + lax.broadcasted_iota(I32, (tm, 1), 0)
    for gi, w in enumerate(POOL_WINDOWS):
        cs = slice(gi * POOL_GROUP_DIM, (gi + 1) * POOL_GROUP_DIM)
        wsum = ext_ref[POOL_HALO:POOL_HALO + tm, cs]
        for jj in range(1, w):
            wsum = wsum + ext_ref[POOL_HALO - jj:POOL_HALO - jj + tm, cs]
        cnt = jnp.minimum(w, pos + 1).astype(F32)
        d = wsum / cnt - x[:, cs]
        y = _dot(d.astype(BF16), w_ref[gi]) * sc_ref[:, cs]
        z_ref[:, cs] = DEEPNORM_ALPHA * x[:, cs] + y
    o_ref[0] = _layer_norm(z_ref[...], g_ref[...], b_ref[...])


def _pool_layer_prompt(x, w_grp, scale, g, b):
    bsz, l, _ = x.shape
    tm = _row_tile(l)
    hb = tm // POOL_HALO
    return pl.pallas_call(
        _pool_prompt_kernel,
        grid=(bsz, l // tm),
        in_specs=[pl.BlockSpec((1, tm, D_MODEL), lambda bi, i: (bi, i, 0)),
                  pl.BlockSpec((1, POOL_HALO, D_MODEL), lambda bi, i: (bi, jnp.maximum(i * hb - 1, 0), 0)),
                  _resident(w_grp.shape), _resident((1, D_MODEL)), _resident((1, D_MODEL)),
                  _resident((1, D_MODEL))],
        out_specs=pl.BlockSpec((1, tm, D_MODEL), lambda bi, i: (bi, i, 0)),
        out_shape=jax.ShapeDtypeStruct((bsz, l, D_MODEL), F32),
        scratch_shapes=[pltpu.VMEM((POOL_HALO + tm, D_MODEL), F32), pltpu.VMEM((tm, D_MODEL), F32)],
        compiler_params=_params("parallel", "arbitrary"),
        name="pool_prompt",
    )(x, x, w_grp, scale, g, b)


def _pool_sample_kernel(x_ref, h_ref, w_ref, sc_ref, g_ref, b_ref, o_ref, z_ref):
    x = x_ref[...]
    for gi, w in enumerate(POOL_WINDOWS):
        cs = slice(gi * POOL_GROUP_DIM, (gi + 1) * POOL_GROUP_DIM)
        wsum = x[:, cs]
        for jj in range(1, w):
            wsum = wsum + h_ref[POOL_HIST - jj, :, cs]
        d = wsum / float(w) - x[:, cs]
        y = _dot(d.astype(BF16), w_ref[gi]) * sc_ref[:, cs]
        z_ref[:, cs] = DEEPNORM_ALPHA * x[:, cs] + y
    o_ref[...] = _layer_norm(z_ref[...], g_ref[...], b_ref[...])


def _pool_layer_sample(x, hist_t, w_grp, scale, g, b):
    db = x.shape[0]
    return pl.pallas_call(
        _pool_sample_kernel,
        grid=(1,),
        in_specs=[_resident(x.shape), _resident(hist_t.shape), _resident(w_grp.shape),
                  _resident((1, D_MODEL)), _resident((1, D_MODEL)), _resident((1, D_MODEL))],
        out_specs=pl.BlockSpec((db, D_MODEL), lambda i: (0, 0)),
        out_shape=jax.ShapeDtypeStruct((db, D_MODEL), F32),
        scratch_shapes=[pltpu.VMEM((db, D_MODEL), F32)],
        compiler_params=_params("arbitrary"),
        name="pool_sample",
    )(x, hist_t, w_grp, scale, g, b)


DSA_QKV = 3 * D_MODEL
DSA_QI_PAD = IDX_HEADS * LANES
DSA_PAD_IN = DSA_QKV + DSA_QI_PAD + LANES
WI_LANE = IDX_DIM


def _dsa_pad_w_in(w_in):
    d = w_in.shape[0]
    qi0 = DSA_QKV
    ki0 = qi0 + IDX_HEADS * IDX_DIM
    z = jnp.zeros((d, LANES - IDX_DIM), w_in.dtype)
    parts = [w_in[:, :DSA_QKV]]
    for h in range(IDX_HEADS):
        parts += [w_in[:, qi0 + h * IDX_DIM:qi0 + (h + 1) * IDX_DIM], z]
    parts += [w_in[:, ki0:], jnp.zeros((d, LANES - IDX_DIM - IDX_HEADS), w_in.dtype)]
    return jnp.concatenate(parts, axis=1)


def _rope_rolled(y, cos, sin, half):
    if 2 * half == LANES:
        rot = pltpu.roll(y, half, 1)
    else:
        lane = lax.broadcasted_iota(I32, y.shape, 1)
        rot = jnp.where(lane < half, pltpu.roll(y, LANES - half, 1), pltpu.roll(y, half, 1))
    return y * cos + rot * sin


def _dsa_proj_kernel(x_ref, w_ref, cq_ref, sq_ref, ci_ref, si_ref,
                     qb_ref, k_ref, v_ref, kb_ref, vb_ref, qi_ref, kiw_ref):
    xb = x_ref[...].astype(BF16)
    cq = cq_ref[...]
    sq = sq_ref[...]
    ci = ci_ref[...]
    si = si_ref[...]
    hd = ATTN_HEAD_DIM
    wide = 4 * hd
    for c in range(0, D_MODEL, wide):
        yq = _dot(xb, w_ref[:, c:c + wide])
        yk = _dot(xb, w_ref[:, D_MODEL + c:D_MODEL + c + wide])
        yv = _dot(xb, w_ref[:, 2 * D_MODEL + c:2 * D_MODEL + c + wide])
        v_ref[:, c:c + wide] = yv
        vb_ref[:, c:c + wide] = yv.astype(BF16)
        for s in range(0, wide, hd):
            cs = slice(c + s, c + s + hd)
            qb_ref[:, cs] = _rope_rolled(yq[:, s:s + hd], cq, sq, hd // 2).astype(BF16)
            kr = _rope_rolled(yk[:, s:s + hd], cq, sq, hd // 2)
            k_ref[:, cs] = kr
            kb_ref[:, cs] = kr.astype(BF16)
    yi = _dot(xb, w_ref[:, DSA_QKV:DSA_QKV + DSA_QI_PAD + LANES])
    for h in range(IDX_HEADS):
        qi_ref[:, h * LANES:(h + 1) * LANES] = _rope_rolled(yi[:, h * LANES:(h + 1) * LANES], ci, si, IDX_DIM // 2)
    kiw_ref[...] = _rope_rolled(yi[:, DSA_QI_PAD:], ci, si, IDX_DIM // 2)


def _dsa_proj(x, w_pad, tabs, pos_tiles):
    m = x.shape[0]
    tm = _row_tile(m)
    tab = pl.BlockSpec((tm, LANES), lambda i: (i % pos_tiles, 0))
    row = lambda n: pl.BlockSpec((tm, n), lambda i: (i, 0))
    sds = lambda n, dt: jax.ShapeDtypeStruct((m, n), dt)
    return pl.pallas_call(
        _dsa_proj_kernel,
        grid=(m // tm,),
        in_specs=[row(D_MODEL), _resident(w_pad.shape), tab, tab, tab, tab],
        out_specs=[row(D_MODEL), row(D_MODEL), row(D_MODEL), row(D_MODEL), row(D_MODEL),
                   row(DSA_QI_PAD), row(LANES)],
        out_shape=[sds(D_MODEL, BF16), sds(D_MODEL, F32), sds(D_MODEL, F32), sds(D_MODEL, BF16),
                   sds(D_MODEL, BF16), sds(DSA_QI_PAD, F32), sds(LANES, F32)],
        compiler_params=_params("parallel"),
        name="dsa_proj",
    )(x, w_pad, *tabs)


F32_ROWS = 8
I16_ROWS = 16
COUNT_CHAINS = 4


def _topk_bias_t(score, kidx, limit, n_top, idx_bits, sc_ref, eq_ref):
    n, q = score.shape
    vis = kidx <= jnp.broadcast_to(jnp.asarray(limit, I32), (n, q))
    sc_ref[0:n, :] = jnp.where(vis, score, NEG_INF)

    def count(pick, slabs):
        chains = min(COUNT_CHAINS, slabs)
        parts = [pick(s) for s in range(chains)]
        for s in range(chains, slabs):
            parts[s % chains] = parts[s % chains] + pick(s)
        part = parts[0]
        for p in parts[1:]:
            part = part + p
        return jnp.sum(part.astype(I32), axis=0, keepdims=True)

    def as_float(key):
        key = jnp.maximum(key, INT32_MIN + 0x7FFFFF)
        return lax.bitcast_convert_type(key ^ ((key >> 31) & 0x7FFFFFFF), F32)

    def count_ge(key):
        c = jnp.broadcast_to(as_float(key), (F32_ROWS, q))
        return count(lambda s: jnp.where(sc_ref[s * F32_ROWS:(s + 1) * F32_ROWS, :] >= c, 1, 0),
                     n // F32_ROWS)

    z = jnp.zeros((1, q), I32)
    t = jnp.where(count_ge(z) >= n_top, z, INT32_MIN)

    def value_bit(i, t):
        cand = t + jnp.left_shift(jnp.int32(1), 30 - i)
        return jnp.where(count_ge(cand) >= n_top, cand, t)

    t = jnp.broadcast_to(as_float(lax.fori_loop(0, 31, value_bit, t)), (n, q))
    sc = sc_ref[0:n, :]
    gt = sc > t
    eq = sc == t
    gt_i = jnp.where(gt, 1, 0)
    need = n_top - count(lambda s: gt_i[s * F32_ROWS:(s + 1) * F32_ROWS, :], n // F32_ROWS)
    eq_ref[0:n, :] = jnp.where(eq, 1, 0).astype(I16)
    zero = jnp.zeros((), I16)
    sub = lax.broadcasted_iota(I16, (I16_ROWS, q), 0)

    def index_bit(i, j):
        cand = j + jnp.left_shift(jnp.int32(1), idx_bits - 1 - i)
        c16 = jnp.broadcast_to(cand.astype(I16), (I16_ROWS, q))
        before = count(lambda s: jnp.where(sub + jnp.int16(s * I16_ROWS) < c16,
                                           eq_ref[s * I16_ROWS:(s + 1) * I16_ROWS, :], zero), n // I16_ROWS)
        return jnp.where(before < need, cand, j)

    j = lax.fori_loop(0, idx_bits, index_bit, z)
    picked = jnp.where(gt, 0.0, jnp.where(eq, jnp.where(kidx <= jnp.broadcast_to(j, (n, q)), 0.0, NEG_INF), NEG_INF))
    return jnp.where(vis, picked, NEG_INF)


def _stack_idx_heads(qi):
    return jnp.concatenate([qi[:, h * LANES:h * LANES + IDX_DIM] for h in range(IDX_HEADS)],
                           axis=0).astype(BF16)


def _dsa_attn_prompt_kernel(q_ref, qi_ref, wt_ref, kiw_ref, kb_ref, vb_ref, w_ref, x_ref, lg_ref, lb_ref,
                            y_ref, sc_ref, eq_ref, o_ref, *, n_top, widths):
    i = pl.program_id(1)
    tq = q_ref.shape[1]
    hd = ATTN_HEAD_DIM

    def attend(n):
        r = _dot_nt(kiw_ref[0, :n, :IDX_DIM].astype(BF16), _stack_idx_heads(qi_ref[0]))
        score = jnp.zeros((n, tq), F32)
        for h in range(IDX_HEADS):
            score = score + (wt_ref[0, h:h + 1, :] * IDX_SCALE) * jnp.maximum(r[:, h * tq:(h + 1) * tq], 0.0)
        kidx = lax.broadcasted_iota(I32, (n, tq), 0)
        pos = i * tq + lax.broadcasted_iota(I32, (1, tq), 1)
        bias = _topk_bias_t(score, kidx, pos, n_top, max(1, (n - 1).bit_length()),
                            sc_ref, eq_ref).T
        for h in range(ATTN_HEADS):
            cs = slice(h * hd, (h + 1) * hd)
            s = _dot_nt(q_ref[0, :, cs], kb_ref[0, :n, cs]) * (hd ** -0.5) + bias
            m = jnp.max(s, axis=-1, keepdims=True)
            p = jnp.exp(s - m)
            l = jnp.sum(p, axis=-1, keepdims=True)
            o_ref[:, cs] = (_dot(p.astype(BF16), vb_ref[0, :n, cs]) / l).astype(BF16)

    last = (i + 1) * tq
    prev = 0
    for n in widths:
        pl.when(jnp.logical_and(last > prev, last <= n))(functools.partial(attend, n))
        prev = n
    y = _dot(o_ref[...], w_ref[...])
    y_ref[0] = _layer_norm(DEEPNORM_ALPHA * x_ref[0] + y, lg_ref[...], lb_ref[...])


def _dsa_layer_prompt(qb, qi, kiw, kb, vb, n_top, w_out, layer, x, ln_g, ln_b):
    b, l, _ = qb.shape
    tq = 256 if l % 256 == 0 else 128
    step = 2 * tq if l % (2 * tq) == 0 else tq
    widths = tuple(range(step, l + 1, step))
    wt = jnp.swapaxes(kiw[:, :, WI_LANE:WI_LANE + IDX_HEADS], 1, 2)
    qrow = lambda n: pl.BlockSpec((1, tq, n), lambda bi, i: (bi, i, 0))
    full = lambda n: pl.BlockSpec((1, l, n), lambda bi, i: (bi, 0, 0))
    return pl.pallas_call(
        functools.partial(_dsa_attn_prompt_kernel, n_top=n_top, widths=widths),
        grid=(b, l // tq),
        in_specs=[qrow(D_MODEL), qrow(DSA_QI_PAD), pl.BlockSpec((1, IDX_HEADS, tq), lambda bi, i: (bi, 0, i)),
                  full(LANES), full(D_MODEL), full(D_MODEL),
                  _resident_layer(w_out, layer), qrow(D_MODEL), _resident((1, D_MODEL)), _resident((1, D_MODEL))],
        out_specs=qrow(D_MODEL),
        out_shape=jax.ShapeDtypeStruct((b, l, D_MODEL), F32),
        scratch_shapes=[pltpu.VMEM((l, tq), F32), pltpu.VMEM((l, tq), I16), pltpu.VMEM((tq, D_MODEL), BF16)],
        compiler_params=_params("parallel", "arbitrary"),
        name="dsa_layer_prompt",
    )(qb, qi, wt, kiw, kb, vb, w_out, x, ln_g, ln_b)


def _dsa_idx_sample_kernel(pt_ref, qi_ref, kiwq_ref, *rest):
    del pt_ref
    pages, o_ref = rest[:-1], rest[-1]
    qs = _stack_idx_heads(qi_ref[0])
    wq = kiwq_ref[0]
    rows = wq.shape[0]
    for p, page_ref in enumerate(pages):
        r = _dot(qs, page_ref[0].astype(BF16))
        acc = jnp.zeros((rows, PAGE_SIZE), F32)
        for h in range(IDX_HEADS):
            acc = acc + (wq[:, WI_LANE + h:WI_LANE + h + 1] * IDX_SCALE) * jnp.maximum(
                r[h * rows:(h + 1) * rows], 0.0)
        o_ref[0, :, p * PAGE_SIZE:(p + 1) * PAGE_SIZE] = acc


def _dsa_idx_sample(page_table, qi, kiw, cache_kidx_t):
    db, n_pages = page_table.shape
    qi8 = jnp.broadcast_to(qi[:, None, :], (db, 8, DSA_QI_PAD))
    kiw8 = jnp.broadcast_to(kiw[:, None, :], (db, 8, LANES))
    page_specs = [pl.BlockSpec((1, IDX_DIM, PAGE_SIZE), functools.partial(lambda p, bi, pt: (pt[bi, p], 0, 0), p))
                  for p in range(n_pages)]
    out = pl.pallas_call(
        _dsa_idx_sample_kernel,
        grid_spec=pltpu.PrefetchScalarGridSpec(
            num_scalar_prefetch=1,
            grid=(db,),
            in_specs=[pl.BlockSpec((1, 8, DSA_QI_PAD), lambda bi, pt: (bi, 0, 0)),
                      pl.BlockSpec((1, 8, LANES), lambda bi, pt: (bi, 0, 0))] + page_specs,
            out_specs=pl.BlockSpec((1, 8, n_pages * PAGE_SIZE), lambda bi, pt: (bi, 0, 0)),
        ),
        out_shape=jax.ShapeDtypeStruct((db, 8, n_pages * PAGE_SIZE), F32),
        compiler_params=_params("arbitrary"),
        name="dsa_idx_sample",
    )(page_table, qi8, kiw8, *([cache_kidx_t] * n_pages))
    return out[:, 0, :]


def _dsa_sel_sample_kernel(spt_ref, qi_ref, kiw_ref, o_ref, sc_ref, eq_ref, *, past, n_top, idx_bits):
    db = qi_ref.shape[0]
    kiw = kiw_ref[...]
    lane = lax.broadcasted_iota(I32, kiw.shape, 1)
    ki = jnp.where(lane < IDX_DIM, kiw, 0.0)
    qi = qi_ref[...]
    self_score = jnp.zeros((db, 1), F32)
    for h in range(IDX_HEADS):
        r = jnp.sum(qi[:, h * LANES:(h + 1) * LANES] * ki, axis=-1, keepdims=True)
        self_score = self_score + (kiw[:, WI_LANE + h:WI_LANE + h + 1] * IDX_SCALE) * jnp.maximum(r, 0.0)
    self_rows = jnp.broadcast_to(self_score, (db, LANES)).T[0:I16_ROWS, :]
    n = past + I16_ROWS
    score = jnp.concatenate([spt_ref[...], self_rows], axis=0)
    kidx = lax.broadcasted_iota(I32, (n, db), 0)
    o_ref[...] = _topk_bias_t(score, kidx, past, n_top, idx_bits, sc_ref, eq_ref)


def _dsa_sel_sample(score_past, qi, kiw, n_top):
    db, past = score_past.shape
    n = past + I16_ROWS
    spt = score_past.T
    out = pl.pallas_call(
        functools.partial(_dsa_sel_sample_kernel, past=past, n_top=n_top, idx_bits=n.bit_length()),
        grid=(1,),
        in_specs=[_resident(spt.shape), _resident(qi.shape), _resident(kiw.shape)],
        out_specs=pl.BlockSpec((n, db), lambda i: (0, 0)),
        out_shape=jax.ShapeDtypeStruct((n, db), F32),
        scratch_shapes=[pltpu.VMEM((n, db), F32), pltpu.VMEM((n, db), I16)],
        compiler_params=_params("arbitrary"),
        name="dsa_sel_sample",
    )(spt, qi, kiw)
    return out.T


def _dsa_attn_sample_kernel(pt_ref, q_ref, kn_ref, vn_ref, bias_ref, bself_ref, *rest, n_pages):
    del pt_ref
    kpages, vpages, o_ref = rest[:n_pages], rest[n_pages:2 * n_pages], rest[-1]
    hd = ATTN_HEAD_DIM
    rows = PAGE_SIZE * ATTN_HEADS
    qb = q_ref[0]
    scale = hd ** -0.5
    head = lax.broadcasted_iota(I32, (ATTN_HEADS, rows), 0)
    col = lax.broadcasted_iota(I32, (ATTN_HEADS, rows), 1)
    own = jnp.bitwise_and(col, ATTN_HEADS - 1) == head
    s_pages = []
    for p, kp in enumerate(kpages):
        k2 = kp[0].reshape(rows, hd).astype(BF16)
        s = _dot_nt(qb, k2) * scale + bias_ref[0, :, p * rows:(p + 1) * rows]
        s_pages.append(jnp.where(own, s, NEG_INF))
    s_self = jnp.sum(qb.astype(F32) * kn_ref[0].astype(BF16).astype(F32), axis=-1, keepdims=True) * scale
    s_self = s_self + bself_ref[0, :, 0:1]
    m = s_self
    for s in s_pages:
        m = jnp.maximum(m, jnp.max(s, axis=-1, keepdims=True))
    p_self = jnp.exp(s_self - m)
    l = p_self
    acc = p_self.astype(BF16).astype(F32) * vn_ref[0].astype(BF16).astype(F32)
    for s, vp in zip(s_pages, vpages):
        pexp = jnp.exp(s - m)
        l = l + jnp.sum(pexp, axis=-1, keepdims=True)
        acc = acc + _dot(pexp.astype(BF16), vp[0].reshape(rows, hd).astype(BF16))
    o_ref[0] = acc / l


def _dsa_attn_sample(page_table, q, k_new, v_new, bias, cache_k, cache_v):
    db, n_pages = page_table.shape
    past = n_pages * PAGE_SIZE
    bias_keys = jnp.repeat(bias[:, :past], ATTN_HEADS, axis=1)[:, None, :]
    bias_self = bias[:, None, past:]
    row = pl.BlockSpec((1, ATTN_HEADS, ATTN_HEAD_DIM), lambda bi, pt: (bi, 0, 0))
    page_spec = lambda p: pl.BlockSpec((1, PAGE_SIZE, ATTN_HEADS, ATTN_HEAD_DIM),
                                       functools.partial(lambda p, bi, pt: (pt[bi, p], 0, 0, 0), p))
    return pl.pallas_call(
        functools.partial(_dsa_attn_sample_kernel, n_pages=n_pages),
        grid_spec=pltpu.PrefetchScalarGridSpec(
            num_scalar_prefetch=1,
            grid=(db,),
            in_specs=[row, row, row,
                      pl.BlockSpec((1, 1, past * ATTN_HEADS), lambda bi, pt: (bi, 0, 0)),
                      pl.BlockSpec((1, 1, bias_self.shape[-1]), lambda bi, pt: (bi, 0, 0))]
                     + [page_spec(p) for p in range(n_pages)] + [page_spec(p) for p in range(n_pages)],
            out_specs=row,
        ),
        out_shape=jax.ShapeDtypeStruct((db, ATTN_HEADS, ATTN_HEAD_DIM), F32),
        compiler_params=_params("arbitrary"),
        name="dsa_attn_sample",
    )(page_table, q, k_new, v_new, bias_keys, bias_self, *([cache_k] * n_pages), *([cache_v] * n_pages))


def kernel(x_prompt, x_sample, state_ret, state_pool, cache_k, cache_v, cache_kidx, page_table,
           ret_w_in, ret_norm_g, ret_w_out, pool_w_grp, pool_scale, dsa_w_in, dsa_w_out,
           ffn_w_in, ffn_w_out, ln_g, ln_b):
    bp, lp, d = x_prompt.shape
    db, ls, _ = x_sample.shape
    assert d == D_MODEL and ls == 1
    past = page_table.shape[1] * PAGE_SIZE
    mp = bp * lp
    tmp = _row_tile(mp)
    assert lp % tmp == 0
    xp = x_prompt.reshape(mp, d)
    xs = x_sample.reshape(db, d)
    pos_p = jnp.arange(lp)
    pos_s = jnp.full((db,), past, jnp.int32)

    ret_w_in_b, ret_w_out_b = ret_w_in.astype(BF16), ret_w_out.astype(BF16)
    ffn_w_in_b, ffn_w_out_b = ffn_w_in.astype(BF16), ffn_w_out.astype(BF16)
    dsa_w_out_b = dsa_w_out.astype(BF16)
    ret_p, pool_p, pool_s = [], [], []
    ret_s = None
    kp_l, vp_l, kip_l, ks_l, vs_l, kis_l = [], [], [], [], [], []
    for i in range(DEPTH):
        kind, j = i % 3, i // 3
        g1, b1 = ln_g[i, 0][None, :], ln_b[i, 0][None, :]
        g2, b2 = ln_g[i, 1][None, :], ln_b[i, 1][None, :]
        if kind == 0:
            ng = ret_norm_g[j][None, :]
            half = RET_KEY_DIM // 2
            cos_p, sin_p = _rope_tables_split(pos_p, half)
            cos_s, sin_s = _rope_tables_split(pos_s, half)
            q, k, v, g = _ret_proj(xp, ret_w_in_b, j, cos_p, sin_p, lp // tmp, BF16)
            shp = lambda a: a.reshape(bp, lp, a.shape[-1])
            xp, sp = _ret_layer_prompt(shp(q), shp(k), shp(v), shp(g), ng, ret_w_out_b, j, shp(xp), g1, b1)
            xp = xp.reshape(mp, d)
            q, k, v, g = _ret_proj(xs, ret_w_in_b, j, cos_s, sin_s, 1, F32)
            o, ret_s = _ret_core_sample(q[:, None, :], k[:, None, :], v[:, None, :], g[:, None, :],
                                        state_ret, j, ng, ret_s)
            xs = _out_ln(o.reshape(db, -1), ret_w_out_b, j, xs, g1, b1)
            ret_p.append(sp)
        elif kind == 1:
            w_grp = pool_w_grp[j].astype(BF16)
            sc = pool_scale[j][None, :]
            pool_p.append(xp.reshape(bp, lp, d)[:, -POOL_HIST:])
            pool_s.append(jnp.concatenate([state_pool[j][:, 1:], xs[:, None, :]], axis=1))
            xp = _pool_layer_prompt(xp.reshape(bp, lp, d), w_grp, sc, g1, b1).reshape(mp, d)
            xs = _pool_layer_sample(xs, jnp.swapaxes(state_pool[j], 0, 1), w_grp, sc, g1, b1)
        else:
            w_pad = _dsa_pad_w_in(dsa_w_in[j]).astype(BF16)
            tabs_p = _rope_tables_rolled(pos_p, ATTN_HEAD_DIM // 2) + _rope_tables_rolled(pos_p, IDX_DIM // 2)
            tabs_s = _rope_tables_rolled(pos_s, ATTN_HEAD_DIM // 2) + _rope_tables_rolled(pos_s, IDX_DIM // 2)
            qb, k32, v32, kb, vb, qi, kiw = _dsa_proj(xp, w_pad, tabs_p, lp // tmp)
            shp = lambda a: a.reshape(bp, lp, a.shape[-1])
            xp = _dsa_layer_prompt(shp(qb), shp(qi), shp(kiw), shp(kb), shp(vb), min(TOPK_MAX, lp // 4),
                                   dsa_w_out_b, j, shp(xp), g1, b1).reshape(mp, d)
            kp_l.append(k32.reshape(bp, lp, ATTN_HEADS, ATTN_HEAD_DIM))
            vp_l.append(v32.reshape(bp, lp, ATTN_HEADS, ATTN_HEAD_DIM))
            kip_l.append(kiw[:, :IDX_DIM].reshape(bp, lp, IDX_DIM))

            qb, k32, v32, kb, vb, qi, kiw = _dsa_proj(xs, w_pad, tabs_s, 1)
            score_past = _dsa_idx_sample(page_table, qi, kiw, jnp.swapaxes(cache_kidx[j], -1, -2))
            bias = _dsa_sel_sample(score_past, qi, kiw, min(TOPK_MAX, (past + ls) // 4))
            heads = lambda a: a.reshape(db, ATTN_HEADS, ATTN_HEAD_DIM)
            o = _dsa_attn_sample(page_table, heads(qb), heads(k32), heads(v32), bias, cache_k[j], cache_v[j])
            xs = _out_ln(o.reshape(db, d), dsa_w_out_b, j, xs, g1, b1)
            ks_l.append(k32.reshape(db, ls, ATTN_HEADS, ATTN_HEAD_DIM))
            vs_l.append(v32.reshape(db, ls, ATTN_HEADS, ATTN_HEAD_DIM))
            kis_l.append(kiw[:, :IDX_DIM].reshape(db, ls, IDX_DIM))
        xp = _ffn_ln(xp, ffn_w_in_b, ffn_w_out_b, i, g2, b2)
        xs = _ffn_ln(xs, ffn_w_in_b, ffn_w_out_b, i, g2, b2)
    return (xp.reshape(bp, lp, d), xs.reshape(db, ls, d), jnp.stack(ret_p), ret_s,
            jnp.stack(pool_p), jnp.stack(pool_s), jnp.stack(kp_l), jnp.stack(vp_l), jnp.stack(kip_l),
            jnp.stack(ks_l), jnp.stack(vs_l), jnp.stack(kis_l))
```

```python
import functools
import math

import jax
import jax.numpy as jnp
from jax import lax
from jax.experimental import pallas as pl
from jax.experimental.pallas import tpu as pltpu

F32 = jnp.float32
BF16 = jnp.bfloat16
I32 = jnp.int32
I16 = jnp.int16

LANES = 128
VMEM_LIMIT_BYTES = 56 * 1024 * 1024

D_MODEL = 1024
DEPTH = 4
PAGE_SIZE = 128
RET_HEADS = 4
RET_KEY_DIM = D_MODEL // RET_HEADS
RET_VAL_DIM = 2 * RET_KEY_DIM
RET_BLOCK = 256
POOL_WINDOWS = (2, 4, 8, 16)
POOL_GROUP_DIM = D_MODEL // len(POOL_WINDOWS)
POOL_HIST = max(POOL_WINDOWS) - 1
ATTN_HEADS = 8
ATTN_HEAD_DIM = D_MODEL // ATTN_HEADS
IDX_HEADS = 4
IDX_DIM = 64
IDX_SCALE = (IDX_HEADS ** -0.5) * (IDX_DIM ** -0.5)
TOPK_MAX = 256
FFN_DIM = -(-8 * D_MODEL // (3 * 256)) * 256
ROPE_THETA = 10000.0
LN_EPS = 1e-5
DEEPNORM_ALPHA = (2.0 * DEPTH) ** 0.25
INT32_MIN = -(2 ** 31)
NEG_INF = float("-inf")


def _params(*sem):
    return pltpu.CompilerParams(dimension_semantics=sem, vmem_limit_bytes=VMEM_LIMIT_BYTES)


def _resident(shape):
    nd = len(shape)
    return pl.BlockSpec(shape, lambda *_: (0,) * nd, pipeline_mode=pl.Buffered(1))


def _resident_layer(stacked, layer):
    shape = stacked.shape[1:]
    nd = len(shape)
    return pl.BlockSpec((None,) + shape, lambda *_: (layer,) + (0,) * nd, pipeline_mode=pl.Buffered(1))


def _row_tile(m):
    for t in (512, 256, 128):
        if m % t == 0:
            return t
    raise ValueError(f"token count {m} is not a multiple of 128")


def _layer_norm(z, g, b):
    mu = jnp.mean(z, axis=-1, keepdims=True)
    zc = z - mu
    var = jnp.mean(zc * zc, axis=-1, keepdims=True)
    return zc * lax.rsqrt(var + LN_EPS) * g + b


def _dot(a, b):
    return jnp.dot(a, b, preferred_element_type=F32)


def _dot_nt(a, b, precision=None):
    return lax.dot_general(a, b, (((1,), (1,)), ((), ())), precision=precision,
                           preferred_element_type=F32)


def _rope_angles(pos, half):
    inv = jnp.power(ROPE_THETA, -jnp.arange(half, dtype=F32) / half)
    return pos.astype(F32)[:, None] * inv[None, :]


def _rope_tables_split(pos, half):
    ang = _rope_angles(pos, half)
    return jnp.cos(ang), jnp.sin(ang)


def _rope_tables_rolled(pos, half, width=LANES):
    ang = _rope_angles(pos, half)
    cos = jnp.concatenate([jnp.cos(ang), jnp.cos(ang)], axis=-1)
    sin = jnp.concatenate([-jnp.sin(ang), jnp.sin(ang)], axis=-1)
    pad = width - 2 * half
    if pad:
        cos = jnp.concatenate([cos, jnp.ones((pos.shape[0], pad), F32)], axis=-1)
        sin = jnp.concatenate([sin, jnp.zeros((pos.shape[0], pad), F32)], axis=-1)
    return cos, sin


def _ret_proj_kernel(x_ref, w_ref, cos_ref, sin_ref, q_ref, k_ref, v_ref, g_ref):
    xb = x_ref[...].astype(BF16)
    cos = cos_ref[...]
    sin = sin_ref[...]
    hk = RET_HEADS * RET_KEY_DIM
    hv = RET_HEADS * RET_VAL_DIM
    half = RET_KEY_DIM // 2
    for h in range(RET_HEADS):
        for base, o_ref, scale in ((0, q_ref, None), (hk, k_ref, RET_KEY_DIM ** -0.5)):
            c0 = base + h * RET_KEY_DIM
            y = _dot(xb, w_ref[:, c0:c0 + RET_KEY_DIM])
            x1 = y[:, :half]
            x2 = y[:, half:]
            o1 = x1 * cos - x2 * sin
            o2 = x2 * cos + x1 * sin
            if scale is not None:
                o1 = o1 * scale
                o2 = o2 * scale
            o_ref[:, h * RET_KEY_DIM:h * RET_KEY_DIM + half] = o1.astype(o_ref.dtype)
            o_ref[:, h * RET_KEY_DIM + half:(h + 1) * RET_KEY_DIM] = o2.astype(o_ref.dtype)
    for h in range(RET_HEADS):
        c0 = h * RET_VAL_DIM
        v_ref[:, c0:c0 + RET_VAL_DIM] = _dot(
            xb, w_ref[:, 2 * hk + c0:2 * hk + c0 + RET_VAL_DIM]).astype(v_ref.dtype)
        g_ref[:, c0:c0 + RET_VAL_DIM] = _dot(
            xb, w_ref[:, 2 * hk + hv + c0:2 * hk + hv + c0 + RET_VAL_DIM]).astype(g_ref.dtype)


def _ret_proj(x, w, layer, cos, sin, pos_tiles, out_dtype):
    m = x.shape[0]
    tm = _row_tile(m)
    hk = RET_HEADS * RET_KEY_DIM
    hv = RET_HEADS * RET_VAL_DIM
    half = RET_KEY_DIM // 2
    tab = pl.BlockSpec((tm, half), lambda i: (i % pos_tiles, 0))
    return pl.pallas_call(
        _ret_proj_kernel,
        grid=(m // tm,),
        in_specs=[pl.BlockSpec((tm, D_MODEL), lambda i: (i, 0)), _resident_layer(w, layer), tab, tab],
        out_specs=[pl.BlockSpec((tm, hk), lambda i: (i, 0)), pl.BlockSpec((tm, hk), lambda i: (i, 0)),
                   pl.BlockSpec((tm, hv), lambda i: (i, 0)), pl.BlockSpec((tm, hv), lambda i: (i, 0))],
        out_shape=[jax.ShapeDtypeStruct((m, hk), out_dtype), jax.ShapeDtypeStruct((m, hk), out_dtype),
                   jax.ShapeDtypeStruct((m, hv), out_dtype), jax.ShapeDtypeStruct((m, hv), out_dtype)],
        compiler_params=_params("parallel"),
        name="ret_proj",
    )(x, w, cos, sin)


def _out_ln_kernel(a_ref, w_ref, x_ref, g_ref, b_ref, o_ref):
    y = _dot(a_ref[...].astype(BF16), w_ref[...])
    o_ref[...] = _layer_norm(DEEPNORM_ALPHA * x_ref[...] + y, g_ref[...], b_ref[...])


def _out_ln(a, w, layer, x, g, b):
    m, ka = a.shape
    tm = _row_tile(m)
    return pl.pallas_call(
        _out_ln_kernel,
        grid=(m // tm,),
        in_specs=[pl.BlockSpec((tm, ka), lambda i: (i, 0)), _resident_layer(w, layer),
                  pl.BlockSpec((tm, D_MODEL), lambda i: (i, 0)),
                  _resident((1, D_MODEL)), _resident((1, D_MODEL))],
        out_specs=pl.BlockSpec((tm, D_MODEL), lambda i: (i, 0)),
        out_shape=jax.ShapeDtypeStruct((m, D_MODEL), F32),
        compiler_params=_params("parallel"),
        name="out_ln",
    )(a, w, x, g, b)


FFN_CHUNK = 256


def _ffn_kernel(x_ref, w_in_ref, w_out_ref, g_ref, b_ref, o_ref, acc_ref):
    x = x_ref[...]
    xb = x.astype(BF16)
    for n, c in enumerate(range(0, FFN_DIM, FFN_CHUNK)):
        a = _dot(xb, w_in_ref[:, c:c + FFN_CHUNK])
        u = _dot(xb, w_in_ref[:, FFN_DIM + c:FFN_DIM + c + FFN_CHUNK])
        hact = (a * jax.nn.sigmoid(a) * u).astype(BF16)
        y = _dot(hact, w_out_ref[c:c + FFN_CHUNK, :])
        if n == 0:
            acc_ref[...] = y
        else:
            acc_ref[...] += y
    o_ref[...] = _layer_norm(DEEPNORM_ALPHA * x + acc_ref[...], g_ref[...], b_ref[...])


def _ffn_ln(x, w_in, w_out, layer, g, b):
    m = x.shape[0]
    tm = _row_tile(m)
    return pl.pallas_call(
        _ffn_kernel,
        grid=(m // tm,),
        in_specs=[pl.BlockSpec((tm, D_MODEL), lambda i: (i, 0)), _resident_layer(w_in, layer),
                  _resident_layer(w_out, layer), _resident((1, D_MODEL)), _resident((1, D_MODEL))],
        out_specs=pl.BlockSpec((tm, D_MODEL), lambda i: (i, 0)),
        out_shape=jax.ShapeDtypeStruct((m, D_MODEL), F32),
        scratch_shapes=[pltpu.VMEM((tm, D_MODEL), F32)],
        compiler_params=_params("parallel"),
        name="ffn_ln",
    )(x, w_in, w_out, g, b)


def _group_norm_gate(o, norm_g, gate):
    mu = jnp.mean(o, axis=-1, keepdims=True)
    oc = o - mu
    var = jnp.mean(oc * oc, axis=-1, keepdims=True)
    on = oc * lax.rsqrt(var + LN_EPS) * norm_g
    return gate * jax.nn.sigmoid(gate) * on


def _ret_decay_tables(c):
    h = RET_HEADS
    lg = jnp.log1p(-jnp.exp2(-5.0 - jnp.arange(h, dtype=F32)))
    i = jnp.arange(c, dtype=F32)
    diff = i[:, None] - i[None, :]
    decay_in = jnp.where(diff[None] >= 0, jnp.exp(jnp.maximum(diff, 0.0)[None] * lg[:, None, None]), 0.0)
    q_dec = jnp.exp((i[None, :] + 1.0) * lg[:, None])
    k_dec = jnp.exp((c - 1.0 - i[None, :]) * lg[:, None])
    c_dec = jnp.exp(c * lg)
    dq = jnp.broadcast_to(q_dec[:, :, None], (h, c, RET_VAL_DIM))
    dk = jnp.broadcast_to(k_dec[:, :, None], (h, c, RET_KEY_DIM))
    dc = jnp.broadcast_to(c_dec[:, None, None], (h, 1, RET_VAL_DIM))
    return decay_in, dq, dk, dc


def _ret_core_kernel(q_ref, k_ref, v_ref, g_ref, din_ref, dq_ref, dk_ref, dc_ref, ng_ref,
                     w_ref, x_ref, lg_ref, lb_ref, *rest):
    y_ref, s_ref, o_ref = rest[-3:]
    c = pl.program_id(1)

    @pl.when(c == 0)
    def _():
        s_ref[...] = jnp.zeros_like(s_ref)

    for h in range(RET_HEADS):
        ks = slice(h * RET_KEY_DIM, (h + 1) * RET_KEY_DIM)
        vs = slice(h * RET_VAL_DIM, (h + 1) * RET_VAL_DIM)
        q = q_ref[0, :, ks]
        k = k_ref[0, :, ks]
        v = v_ref[0, :, vs]
        s = s_ref[0, 0, h]
        att = _dot_nt(q, k) * din_ref[h]
        inner = _dot(att.astype(BF16), v)
        cross = _dot(q, s.astype(BF16)) * dq_ref[h]
        kt = (k.astype(F32) * dk_ref[h]).T.astype(BF16)
        s_ref[0, 0, h] = s * dc_ref[h] + _dot(kt, v)
        o_ref[:, vs] = _group_norm_gate(inner + cross, ng_ref[:, vs],
                                        g_ref[0, :, vs].astype(F32)).astype(BF16)
    y = _dot(o_ref[...], w_ref[...])
    y_ref[0] = _layer_norm(DEEPNORM_ALPHA * x_ref[0] + y, lg_ref[...], lb_ref[...])


def _ret_layer_prompt(q, k, v, g, norm_g, w_out, layer, x, ln_g, ln_b, states):
    b, l, _ = q.shape
    c = math.gcd(l, RET_BLOCK)
    hk = RET_HEADS * RET_KEY_DIM
    hv = RET_HEADS * RET_VAL_DIM
    decay_in, dq, dk, dc = _ret_decay_tables(c)
    kspec = pl.BlockSpec((1, c, hk), lambda bi, ci: (bi, ci, 0))
    vspec = pl.BlockSpec((1, c, hv), lambda bi, ci: (bi, ci, 0))
    xspec = pl.BlockSpec((1, c, D_MODEL), lambda bi, ci: (bi, ci, 0))
    in_specs = [kspec, kspec, vspec, vspec, _resident(decay_in.shape), _resident(dq.shape),
                _resident(dk.shape), _resident(dc.shape), _resident(norm_g.shape),
                _resident_layer(w_out, layer), xspec, _resident((1, D_MODEL)), _resident((1, D_MODEL))]
    args = [q, k, v, g, decay_in, dq, dk, dc, norm_g, w_out, x, ln_g, ln_b]
    aliases = {}
    if states is not None:
        in_specs.append(pl.BlockSpec(memory_space=pl.ANY))
        args.append(states)
        aliases = {len(args) - 1: 1}
    return pl.pallas_call(
        _ret_core_kernel,
        grid=(b, l // c),
        in_specs=in_specs,
        out_specs=[xspec,
                   pl.BlockSpec((1, 1, RET_HEADS, RET_KEY_DIM, RET_VAL_DIM), lambda bi, ci: (layer, bi, 0, 0, 0))],
        out_shape=[jax.ShapeDtypeStruct((b, l, D_MODEL), F32),
                   jax.ShapeDtypeStruct((w_out.shape[0], b, RET_HEADS, RET_KEY_DIM, RET_VAL_DIM), F32)],
        scratch_shapes=[pltpu.VMEM((c, hv), BF16)],
        input_output_aliases=aliases,
        compiler_params=_params("parallel", "arbitrary"),
        name="ret_layer",
    )(*args)


def _column(row, n):
    r = lax.broadcasted_iota(I32, (n, n), 0)
    c = lax.broadcasted_iota(I32, (n, n), 1)
    return jnp.sum(jnp.where(r == c, jnp.broadcast_to(row, (n, n)), 0.0), axis=1, keepdims=True)


def _ret_sample_kernel(q_ref, k_ref, v_ref, g_ref, s_ref, gam_ref, ng_ref, *rest):
    o_ref, so_ref = rest[-2:]
    for r in range(q_ref.shape[0]):
        for h in range(RET_HEADS):
            ks = slice(h * RET_KEY_DIM, (h + 1) * RET_KEY_DIM)
            vs = slice(h * RET_VAL_DIM, (h + 1) * RET_VAL_DIM)
            qcol = _column(q_ref[r, :, ks], RET_KEY_DIM)
            kcol = _column(k_ref[r, :, ks], RET_KEY_DIM)
            v = v_ref[r, :, vs]
            s_new = s_ref[0, r, h] * gam_ref[h] + kcol * v
            so_ref[0, r, h] = s_new
            o = jnp.sum(qcol * s_new, axis=0, keepdims=True)
            o_ref[r, :, vs] = _group_norm_gate(o, ng_ref[:, vs], g_ref[r, :, vs])


def _ret_core_sample(q, k, v, g, state_ret, j, norm_g, new_states):
    db = q.shape[0]
    lg = jnp.log1p(-jnp.exp2(-5.0 - jnp.arange(RET_HEADS, dtype=F32)))
    gam = jnp.broadcast_to(jnp.exp(lg)[:, None, None], (RET_HEADS, 1, RET_VAL_DIM))
    tb = 4 if db % 4 == 0 else 1
    kspec = pl.BlockSpec((tb, 1, RET_HEADS * RET_KEY_DIM), lambda bi: (bi, 0, 0))
    vspec = pl.BlockSpec((tb, 1, RET_HEADS * RET_VAL_DIM), lambda bi: (bi, 0, 0))
    sspec = pl.BlockSpec((1, tb, RET_HEADS, RET_KEY_DIM, RET_VAL_DIM), lambda bi: (j, bi, 0, 0, 0))
    in_specs = [kspec, kspec, vspec, vspec, sspec, _resident(gam.shape), _resident(norm_g.shape)]
    args = [q, k, v, g, state_ret, gam, norm_g]
    aliases = {}
    if new_states is not None:
        in_specs.append(pl.BlockSpec(memory_space=pl.ANY))
        args.append(new_states)
        aliases = {len(args) - 1: 1}
    return pl.pallas_call(
        _ret_sample_kernel,
        grid=(db // tb,),
        in_specs=in_specs,
        out_specs=[vspec, sspec],
        out_shape=[jax.ShapeDtypeStruct((db, 1, RET_HEADS * RET_VAL_DIM), F32),
                   jax.ShapeDtypeStruct(state_ret.shape, F32)],
        input_output_aliases=aliases,
        compiler_params=_params("parallel"),
        name="ret_sample",
    )(*args)


POOL_HALO = 16


def _pool_prompt_kernel(x_ref, xh_ref, w_ref, sc_ref, g_ref, b_ref, o_ref, ext_ref, z_ref):
    i = pl.program_id(1)
    tm = x_ref.shape[1]
    x = x_ref[0]
    halo = jnp.where(i == 0, 0.0, xh_ref[0])
    ext_ref[0:POOL_HALO, :] = halo
    ext_ref[POOL_HALO:POOL_HALO + tm, :] = x
    pos = i * tm + lax.broadcasted_iota(I32, (tm, 1), 0)
    for gi, w in enumerate(POOL_WINDOWS):
        cs = slice(gi * POOL_GROUP_DIM, (gi + 1) * POOL_GROUP_DIM)
        wsum = ext_ref[POOL_HALO:POOL_HALO + tm, cs]
        for jj in range(1, w):
            wsum = wsum + ext_ref[POOL_HALO - jj:POOL_HALO - jj + tm, cs]
        cnt = jnp.minimum(w, pos + 1).astype(F32)
        d = wsum / cnt - x[:, cs]
        y = _dot(d.astype(BF16), w_ref[gi]) * sc_ref[:, cs]
        z_ref[:, cs] = DEEPNORM_ALPHA * x[:, cs] + y
    o_ref[0] = _layer_norm(z_ref[...], g_ref[...], b_ref[...])


def _pool_layer_prompt(x, w_grp, scale, g, b):
    bsz, l, _ = x.shape
    tm = _row_tile(l)
    hb = tm // POOL_HALO
    return pl.pallas_call(
        _pool_prompt_kernel,
        grid=(bsz, l // tm),
        in_specs=[pl.BlockSpec((1, tm, D_MODEL), lambda bi, i: (bi, i, 0)),
                  pl.BlockSpec((1, POOL_HALO, D_MODEL), lambda bi, i: (bi, jnp.maximum(i * hb - 1, 0), 0)),
                  _resident(w_grp.shape), _resident((1, D_MODEL)), _resident((1, D_MODEL)),
                  _resident((1, D_MODEL))],
        out_specs=pl.BlockSpec((1, tm, D_MODEL), lambda bi, i: (bi, i, 0)),
        out_shape=jax.ShapeDtypeStruct((bsz, l, D_MODEL), F32),
        scratch_shapes=[pltpu.VMEM((POOL_HALO + tm, D_MODEL), F32), pltpu.VMEM((tm, D_MODEL), F32)],
        compiler_params=_params("parallel", "arbitrary"),
        name="pool_prompt",
    )(x, x, w_grp, scale, g, b)


def _pool_sample_kernel(x_ref, h_ref, w_ref, sc_ref, g_ref, b_ref, o_ref, z_ref):
    x = x_ref[...]
    for gi, w in enumerate(POOL_WINDOWS):
        cs = slice(gi * POOL_GROUP_DIM, (gi + 1) * POOL_GROUP_DIM)
        wsum = x[:, cs]
        for jj in range(1, w):
            wsum = wsum + h_ref[POOL_HIST - jj, :, cs]
        d = wsum / float(w) - x[:, cs]
        y = _dot(d.astype(BF16), w_ref[gi]) * sc_ref[:, cs]
        z_ref[:, cs] = DEEPNORM_ALPHA * x[:, cs] + y
    o_ref[...] = _layer_norm(z_ref[...], g_ref[...], b_ref[...])


def _pool_layer_sample(x, hist_t, w_grp, scale, g, b):
    db = x.shape[0]
    return pl.pallas_call(
        _pool_sample_kernel,
        grid=(1,),
        in_specs=[_resident(x.shape), _resident(hist_t.shape), _resident(w_grp.shape),
                  _resident((1, D_MODEL)), _resident((1, D_MODEL)), _resident((1, D_MODEL))],
        out_specs=pl.BlockSpec((db, D_MODEL), lambda i: (0, 0)),
        out_shape=jax.ShapeDtypeStruct((db, D_MODEL), F32),
        scratch_shapes=[pltpu.VMEM((db, D_MODEL), F32)],
        compiler_params=_params("arbitrary"),
        name="pool_sample",
    )(x, hist_t, w_grp, scale, g, b)


DSA_QKV = 3 * D_MODEL
DSA_QI_PAD = IDX_HEADS * LANES
DSA_PAD_IN = DSA_QKV + DSA_QI_PAD + LANES
WI_LANE = IDX_DIM


def _dsa_pad_w_in(w_in):
    d = w_in.shape[0]
    qi0 = DSA_QKV
    ki0 = qi0 + IDX_HEADS * IDX_DIM
    z = jnp.zeros((d, LANES - IDX_DIM), w_in.dtype)
    parts = [w_in[:, :DSA_QKV]]
    for h in range(IDX_HEADS):
        parts += [w_in[:, qi0 + h * IDX_DIM:qi0 + (h + 1) * IDX_DIM], z]
    parts += [w_in[:, ki0:], jnp.zeros((d, LANES - IDX_DIM - IDX_HEADS), w_in.dtype)]
    return jnp.concatenate(parts, axis=1)


def _rope_rolled(y, cos, sin, half):
    if 2 * half == LANES:
        rot = pltpu.roll(y, half, 1)
    else:
        lane = lax.broadcasted_iota(I32, y.shape, 1)
        rot = jnp.where(lane < half, pltpu.roll(y, LANES - half, 1), pltpu.roll(y, half, 1))
    return y * cos + rot * sin


def _dsa_proj_kernel(x_ref, w_ref, cq_ref, sq_ref, ci_ref, si_ref,
                     qb_ref, k_ref, v_ref, kb_ref, vb_ref, qi_ref, kiw_ref):
    xb = x_ref[...].astype(BF16)
    cq = cq_ref[...]
    sq = sq_ref[...]
    ci = ci_ref[...]
    si = si_ref[...]
    hd = ATTN_HEAD_DIM
    wide = 4 * hd
    for c in range(0, D_MODEL, wide):
        yq = _dot(xb, w_ref[:, c:c + wide])
        yk = _dot(xb, w_ref[:, D_MODEL + c:D_MODEL + c + wide])
        yv = _dot(xb, w_ref[:, 2 * D_MODEL + c:2 * D_MODEL + c + wide])
        v_ref[:, c:c + wide] = yv
        vb_ref[:, c:c + wide] = yv.astype(BF16)
        for s in range(0, wide, hd):
            cs = slice(c + s, c + s + hd)
            qb_ref[:, cs] = _rope_rolled(yq[:, s:s + hd], cq, sq, hd // 2).astype(BF16)
            kr = _rope_rolled(yk[:, s:s + hd], cq, sq, hd // 2)
            k_ref[:, cs] = kr
            kb_ref[:, cs] = kr.astype(BF16)
    yi = _dot(xb, w_ref[:, DSA_QKV:DSA_QKV + DSA_QI_PAD + LANES])
    for h in range(IDX_HEADS):
        qi_ref[:, h * LANES:(h + 1) * LANES] = _rope_rolled(yi[:, h * LANES:(h + 1) * LANES], ci, si, IDX_DIM // 2)
    kiw_ref[...] = _rope_rolled(yi[:, DSA_QI_PAD:], ci, si, IDX_DIM // 2)


def _dsa_proj(x, w_pad, tabs, pos_tiles):
    m = x.shape[0]
    tm = _row_tile(m)
    tab = pl.BlockSpec((tm, LANES), lambda i: (i % pos_tiles, 0))
    row = lambda n: pl.BlockSpec((tm, n), lambda i: (i, 0))
    sds = lambda n, dt: jax.ShapeDtypeStruct((m, n), dt)
    return pl.pallas_call(
        _dsa_proj_kernel,
        grid=(m // tm,),
        in_specs=[row(D_MODEL), _resident(w_pad.shape), tab, tab, tab, tab],
        out_specs=[row(D_MODEL), row(D_MODEL), row(D_MODEL), row(D_MODEL), row(D_MODEL),
                   row(DSA_QI_PAD), row(LANES)],
        out_shape=[sds(D_MODEL, BF16), sds(D_MODEL, F32), sds(D_MODEL, F32), sds(D_MODEL, BF16),
                   sds(D_MODEL, BF16), sds(DSA_QI_PAD, F32), sds(LANES, F32)],
        compiler_params=_params("parallel"),
        name="dsa_proj",
    )(x, w_pad, *tabs)


F32_ROWS = 8
I16_ROWS = 16
COUNT_CHAINS = 4


def _topk_bias_t(score, kidx, limit, n_top, idx_bits, sc_ref, eq_ref):
    n, q = score.shape
    vis = kidx <= jnp.broadcast_to(jnp.asarray(limit, I32), (n, q))
    sc_ref[0:n, :] = jnp.where(vis, score, NEG_INF)

    def count(pick, slabs):
        chains = min(COUNT_CHAINS, slabs)
        parts = [pick(s) for s in range(chains)]
        for s in range(chains, slabs):
            parts[s % chains] = parts[s % chains] + pick(s)
        part = parts[0]
        for p in parts[1:]:
            part = part + p
        return jnp.sum(part.astype(I32), axis=0, keepdims=True)

    def as_float(key):
        key = jnp.maximum(key, INT32_MIN + 0x7FFFFF)
        return lax.bitcast_convert_type(key ^ ((key >> 31) & 0x7FFFFFFF), F32)

    def count_ge(key):
        c = jnp.broadcast_to(as_float(key), (F32_ROWS, q))
        return count(lambda s: jnp.where(sc_ref[s * F32_ROWS:(s + 1) * F32_ROWS, :] >= c, 1, 0),
                     n // F32_ROWS)

    z = jnp.zeros((1, q), I32)
    t = jnp.where(count_ge(z) >= n_top, z, INT32_MIN)

    def value_bit(i, t):
        cand = t + jnp.left_shift(jnp.int32(1), 30 - i)
        return jnp.where(count_ge(cand) >= n_top, cand, t)

    t = jnp.broadcast_to(as_float(lax.fori_loop(0, 31, value_bit, t)), (n, q))
    sc = sc_ref[0:n, :]
    gt = sc > t
    eq = sc == t
    gt_i = jnp.where(gt, 1, 0)
    need = n_top - count(lambda s: gt_i[s * F32_ROWS:(s + 1) * F32_ROWS, :], n // F32_ROWS)
    eq_ref[0:n, :] = jnp.where(eq, 1, 0).astype(I16)
    zero = jnp.zeros((), I16)
    sub = lax.broadcasted_iota(I16, (I16_ROWS, q), 0)

    def index_bit(i, j):
        cand = j + jnp.left_shift(jnp.int32(1), idx_bits - 1 - i)
        c16 = jnp.broadcast_to(cand.astype(I16), (I16_ROWS, q))
        before = count(lambda s: jnp.where(sub + jnp.int16(s * I16_ROWS) < c16,
                                           eq_ref[s * I16_ROWS:(s + 1) * I16_ROWS, :], zero), n // I16_ROWS)
        return jnp.where(before < need, cand, j)

    j = lax.fori_loop(0, idx_bits, index_bit, z)
    picked = jnp.where(gt, 0.0, jnp.where(eq, jnp.where(kidx <= jnp.broadcast_to(j, (n, q)), 0.0, NEG_INF), NEG_INF))
    return jnp.where(vis, picked, NEG_INF)


def _stack_idx_heads(qi):
    return jnp.concatenate([qi[:, h * LANES:h * LANES + IDX_DIM] for h in range(IDX_HEADS)],
                           axis=0).astype(BF16)


def _dsa_attn_prompt_kernel(q_ref, qi_ref, wt_ref, kiw_ref, kb_ref, vb_ref, w_ref, x_ref, lg_ref, lb_ref,
                            y_ref, sc_ref, eq_ref, o_ref, *, n_top, widths):
    i = pl.program_id(1)
    tq = q_ref.shape[1]
    hd = ATTN_HEAD_DIM

    def attend(n):
        r = _dot_nt(kiw_ref[0, :n, :IDX_DIM].astype(BF16), _stack_idx_heads(qi_ref[0]))
        score = jnp.zeros((n, tq), F32)
        for h in range(IDX_HEADS):
            score = score + (wt_ref[0, h:h + 1, :] * IDX_SCALE) * jnp.maximum(r[:, h * tq:(h + 1) * tq], 0.0)
        kidx = lax.broadcasted_iota(I32, (n, tq), 0)
        pos = i * tq + lax.broadcasted_iota(I32, (1, tq), 1)
        bias = _topk_bias_t(score, kidx, pos, n_top, max(1, (n - 1).bit_length()),
                            sc_ref, eq_ref).T
        for h in range(ATTN_HEADS):
            cs = slice(h * hd, (h + 1) * hd)
            s = _dot_nt(q_ref[0, :, cs], kb_ref[0, :n, cs]) * (hd ** -0.5) + bias
            m = jnp.max(s, axis=-1, keepdims=True)
            p = jnp.exp(s - m)
            l = jnp.sum(p, axis=-1, keepdims=True)
            o_ref[:, cs] = (_dot(p.astype(BF16), vb_ref[0, :n, cs]) / l).astype(BF16)

    last = (i + 1) * tq
    prev = 0
    for n in widths:
        pl.when(jnp.logical_and(last > prev, last <= n))(functools.partial(attend, n))
        prev = n
    y = _dot(o_ref[...], w_ref[...])
    y_ref[0] = _layer_norm(DEEPNORM_ALPHA * x_ref[0] + y, lg_ref[...], lb_ref[...])


def _dsa_layer_prompt(qb, qi, kiw, kb, vb, n_top, w_out, layer, x, ln_g, ln_b):
    b, l, _ = qb.shape
    tq = 256 if l % 256 == 0 else 128
    step = 2 * tq if l % (2 * tq) == 0 else tq
    widths = tuple(range(step, l + 1, step))
    wt = jnp.swapaxes(kiw[:, :, WI_LANE:WI_LANE + IDX_HEADS], 1, 2)
    qrow = lambda n: pl.BlockSpec((1, tq, n), lambda bi, i: (bi, i, 0))
    full = lambda n: pl.BlockSpec((1, l, n), lambda bi, i: (bi, 0, 0))
    return pl.pallas_call(
        functools.partial(_dsa_attn_prompt_kernel, n_top=n_top, widths=widths),
        grid=(b, l // tq),
        in_specs=[qrow(D_MODEL), qrow(DSA_QI_PAD), pl.BlockSpec((1, IDX_HEADS, tq), lambda bi, i: (bi, 0, i)),
                  full(LANES), full(D_MODEL), full(D_MODEL),
                  _resident_layer(w_out, layer), qrow(D_MODEL), _resident((1, D_MODEL)), _resident((1, D_MODEL))],
        out_specs=qrow(D_MODEL),
        out_shape=jax.ShapeDtypeStruct((b, l, D_MODEL), F32),
        scratch_shapes=[pltpu.VMEM((l, tq), F32), pltpu.VMEM((l, tq), I16), pltpu.VMEM((tq, D_MODEL), BF16)],
        compiler_params=_params("parallel", "arbitrary"),
        name="dsa_layer_prompt",
    )(qb, qi, wt, kiw, kb, vb, w_out, x, ln_g, ln_b)


def _dsa_idx_sample_kernel(pt_ref, qi_ref, kiwq_ref, *rest):
    del pt_ref
    pages, o_ref = rest[:-1], rest[-1]
    qs = _stack_idx_heads(qi_ref[0])
    wq = kiwq_ref[0]
    rows = wq.shape[0]
    for p, page_ref in enumerate(pages):
        r = _dot(qs, page_ref[0].astype(BF16))
        acc = jnp.zeros((rows, PAGE_SIZE), F32)
        for h in range(IDX_HEADS):
            acc = acc + (wq[:, WI_LANE + h:WI_LANE + h + 1] * IDX_SCALE) * jnp.maximum(
                r[h * rows:(h + 1) * rows], 0.0)
        o_ref[0, :, p * PAGE_SIZE:(p + 1) * PAGE_SIZE] = acc


def _dsa_idx_sample(page_table, qi, kiw, cache_kidx_t):
    db, n_pages = page_table.shape
    qi8 = jnp.broadcast_to(qi[:, None, :], (db, 8, DSA_QI_PAD))
    kiw8 = jnp.broadcast_to(kiw[:, None, :], (db, 8, LANES))
    page_specs = [pl.BlockSpec((1, IDX_DIM, PAGE_SIZE), functools.partial(lambda p, bi, pt: (pt[bi, p], 0, 0), p))
                  for p in range(n_pages)]
    out = pl.pallas_call(
        _dsa_idx_sample_kernel,
        grid_spec=pltpu.PrefetchScalarGridSpec(
            num_scalar_prefetch=1,
            grid=(db,),
            in_specs=[pl.BlockSpec((1, 8, DSA_QI_PAD), lambda bi, pt: (bi, 0, 0)),
                      pl.BlockSpec((1, 8, LANES), lambda bi, pt: (bi, 0, 0))] + page_specs,
            out_specs=pl.BlockSpec((1, 8, n_pages * PAGE_SIZE), lambda bi, pt: (bi, 0, 0)),
        ),
        out_shape=jax.ShapeDtypeStruct((db, 8, n_pages * PAGE_SIZE), F32),
        compiler_params=_params("arbitrary"),
        name="dsa_idx_sample",
    )(page_table, qi8, kiw8, *([cache_kidx_t] * n_pages))
    return out[:, 0, :]


def _dsa_sel_sample_kernel(spt_ref, qi_ref, kiw_ref, o_ref, sc_ref, eq_ref, *, past, n_top, idx_bits):
    db = qi_ref.shape[0]
    kiw = kiw_ref[...]
    lane = lax.broadcasted_iota(I32, kiw.shape, 1)
    ki = jnp.where(lane < IDX_DIM, kiw, 0.0)
    qi = qi_ref[...]
    self_score = jnp.zeros((db, 1), F32)
    for h in range(IDX_HEADS):
        r = jnp.sum(qi[:, h * LANES:(h + 1) * LANES] * ki, axis=-1, keepdims=True)
        self_score = self_score + (kiw[:, WI_LANE + h:WI_LANE + h + 1] * IDX_SCALE) * jnp.maximum(r, 0.0)
    self_rows = jnp.broadcast_to(self_score, (db, LANES)).T[0:I16_ROWS, :]
    n = past + I16_ROWS
    score = jnp.concatenate([spt_ref[...], self_rows], axis=0)
    kidx = lax.broadcasted_iota(I32, (n, db), 0)
    o_ref[...] = _topk_bias_t(score, kidx, past, n_top, idx_bits, sc_ref, eq_ref)


def _dsa_sel_sample(score_past, qi, kiw, n_top):
    db, past = score_past.shape
    n = past + I16_ROWS
    spt = score_past.T
    out = pl.pallas_call(
        functools.partial(_dsa_sel_sample_kernel, past=past, n_top=n_top, idx_bits=n.bit_length()),
        grid=(1,),
        in_specs=[_resident(spt.shape), _resident(qi.shape), _resident(kiw.shape)],
        out_specs=pl.BlockSpec((n, db), lambda i: (0, 0)),
        out_shape=jax.ShapeDtypeStruct((n, db), F32),
        scratch_shapes=[pltpu.VMEM((n, db), F32), pltpu.VMEM((n, db), I16)],
        compiler_params=_params("arbitrary"),
        name="dsa_sel_sample",
    )(spt, qi, kiw)
    return out.T


def _dsa_attn_sample_kernel(pt_ref, q_ref, kn_ref, vn_ref, bias_ref, bself_ref, *rest, n_pages):
    del pt_ref
    kpages, vpages, o_ref = rest[:n_pages], rest[n_pages:2 * n_pages], rest[-1]
    hd = ATTN_HEAD_DIM
    rows = PAGE_SIZE * ATTN_HEADS
    qb = q_ref[0]
    scale = hd ** -0.5
    head = lax.broadcasted_iota(I32, (ATTN_HEADS, rows), 0)
    col = lax.broadcasted_iota(I32, (ATTN_HEADS, rows), 1)
    own = jnp.bitwise_and(col, ATTN_HEADS - 1) == head
    s_pages = []
    for p, kp in enumerate(kpages):
        k2 = kp[0].reshape(rows, hd).astype(BF16)
        s = _dot_nt(qb, k2) * scale + bias_ref[0, :, p * rows:(p + 1) * rows]
        s_pages.append(jnp.where(own, s, NEG_INF))
    s_self = jnp.sum(qb.astype(F32) * kn_ref[0].astype(BF16).astype(F32), axis=-1, keepdims=True) * scale
    s_self = s_self + bself_ref[0, :, 0:1]
    m = s_self
    for s in s_pages:
        m = jnp.maximum(m, jnp.max(s, axis=-1, keepdims=True))
    p_self = jnp.exp(s_self - m)
    l = p_self
    acc = p_self.astype(BF16).astype(F32) * vn_ref[0].astype(BF16).astype(F32)
    for s, vp in zip(s_pages, vpages):
        pexp = jnp.exp(s - m)
        l = l + jnp.sum(pexp, axis=-1, keepdims=True)
        acc = acc + _dot(pexp.astype(BF16), vp[0].reshape(rows, hd).astype(BF16))
    o_ref[0] = acc / l


def _dsa_attn_sample(page_table, q, k_new, v_new, bias, cache_k, cache_v):
    db, n_pages = page_table.shape
    past = n_pages * PAGE_SIZE
    bias_keys = jnp.repeat(bias[:, :past], ATTN_HEADS, axis=1)[:, None, :]
    bias_self = bias[:, None, past:]
    row = pl.BlockSpec((1, ATTN_HEADS, ATTN_HEAD_DIM), lambda bi, pt: (bi, 0, 0))
    page_spec = lambda p: pl.BlockSpec((1, PAGE_SIZE, ATTN_HEADS, ATTN_HEAD_DIM),
                                       functools.partial(lambda p, bi, pt: (pt[bi, p], 0, 0, 0), p))
    return pl.pallas_call(
        functools.partial(_dsa_attn_sample_kernel, n_pages=n_pages),
        grid_spec=pltpu.PrefetchScalarGridSpec(
            num_scalar_prefetch=1,
            grid=(db,),
            in_specs=[row, row, row,
                      pl.BlockSpec((1, 1, past * ATTN_HEADS), lambda bi, pt: (bi, 0, 0)),
                      pl.BlockSpec((1, 1, bias_self.shape[-1]), lambda bi, pt: (bi, 0, 0))]
                     + [page_spec(p) for p in range(n_pages)] + [page_spec(p) for p in range(n_pages)],
            out_specs=row,
        ),
        out_shape=jax.ShapeDtypeStruct((db, ATTN_HEADS, ATTN_HEAD_DIM), F32),
        compiler_params=_params("arbitrary"),
        name="dsa_attn_sample",
    )(page_table, q, k_new, v_new, bias_keys, bias_self, *([cache_k] * n_pages), *([cache_v] * n_pages))


def kernel(x_prompt, x_sample, state_ret, state_pool, cache_k, cache_v, cache_kidx, page_table,
           ret_w_in, ret_norm_g, ret_w_out, pool_w_grp, pool_scale, dsa_w_in, dsa_w_out,
           ffn_w_in, ffn_w_out, ln_g, ln_b):
    bp, lp, d = x_prompt.shape
    db, ls, _ = x_sample.shape
    assert d == D_MODEL and ls == 1
    past = page_table.shape[1] * PAGE_SIZE
    mp = bp * lp
    tmp = _row_tile(mp)
    assert lp % tmp == 0
    xp = x_prompt.reshape(mp, d)
    xs = x_sample.reshape(db, d)
    pos_p = jnp.arange(lp)
    pos_s = jnp.full((db,), past, jnp.int32)

    ret_w_in_b, ret_w_out_b = ret_w_in.astype(BF16), ret_w_out.astype(BF16)
    ffn_w_in_b, ffn_w_out_b = ffn_w_in.astype(BF16), ffn_w_out.astype(BF16)
    dsa_w_out_b = dsa_w_out.astype(BF16)
    pool_p, pool_s = [], []
    ret_p = None
    ret_s = None
    kp_l, vp_l, kip_l, ks_l, vs_l, kis_l = [], [], [], [], [], []
    for i in range(DEPTH):
        kind, j = i % 3, i // 3
        g1, b1 = ln_g[i, 0][None, :], ln_b[i, 0][None, :]
        g2, b2 = ln_g[i, 1][None, :], ln_b[i, 1][None, :]
        if kind == 0:
            ng = ret_norm_g[j][None, :]
            half = RET_KEY_DIM // 2
            cos_p, sin_p = _rope_tables_split(pos_p, half)
            cos_s, sin_s = _rope_tables_split(pos_s, half)
            q, k, v, g = _ret_proj(xp, ret_w_in_b, j, cos_p, sin_p, lp // tmp, BF16)
            shp = lambda a: a.reshape(bp, lp, a.shape[-1])
            xp, ret_p = _ret_layer_prompt(shp(q), shp(k), shp(v), shp(g), ng, ret_w_out_b, j, shp(xp), g1, b1, ret_p)
            xp = xp.reshape(mp, d)
            q, k, v, g = _ret_proj(xs, ret_w_in_b, j, cos_s, sin_s, 1, F32)
            o, ret_s = _ret_core_sample(q[:, None, :], k[:, None, :], v[:, None, :], g[:, None, :],
                                        state_ret, j, ng, ret_s)
            xs = _out_ln(o.reshape(db, -1), ret_w_out_b, j, xs, g1, b1)
        elif kind == 1:
            w_grp = pool_w_grp[j].astype(BF16)
            sc = pool_scale[j][None, :]
            pool_p.append(xp.reshape(bp, lp, d)[:, -POOL_HIST:])
            pool_s.append(jnp.concatenate([state_pool[j][:, 1:], xs[:, None, :]], axis=1))
            xp = _pool_layer_prompt(xp.reshape(bp, lp, d), w_grp, sc, g1, b1).reshape(mp, d)
            xs = _pool_layer_sample(xs, jnp.swapaxes(state_pool[j], 0, 1), w_grp, sc, g1, b1)
        else:
            w_pad = _dsa_pad_w_in(dsa_w_in[j]).astype(BF16)
            tabs_p = _rope_tables_rolled(pos_p, ATTN_HEAD_DIM // 2) + _rope_tables_rolled(pos_p, IDX_DIM // 2)
            tabs_s = _rope_tables_rolled(pos_s, ATTN_HEAD_DIM // 2) + _rope_tables_rolled(pos_s, IDX_DIM // 2)
            qb, k32, v32, kb, vb, qi, kiw = _dsa_proj(xp, w_pad, tabs_p, lp // tmp)
            shp = lambda a: a.reshape(bp, lp, a.shape[-1])
            xp = _dsa_layer_prompt(shp(qb), shp(qi), shp(kiw), shp(kb), shp(vb), min(TOPK_MAX, lp // 4),
                                   dsa_w_out_b, j, shp(xp), g1, b1).reshape(mp, d)
            kp_l.append(k32.reshape(bp, lp, ATTN_HEADS, ATTN_HEAD_DIM))
            vp_l.append(v32.reshape(bp, lp, ATTN_HEADS, ATTN_HEAD_DIM))
            kip_l.append(kiw[:, :IDX_DIM].reshape(bp, lp, IDX_DIM))

            qb, k32, v32, kb, vb, qi, kiw = _dsa_proj(xs, w_pad, tabs_s, 1)
            score_past = _dsa_idx_sample(page_table, qi, kiw, jnp.swapaxes(cache_kidx[j], -1, -2))
            bias = _dsa_sel_sample(score_past, qi, kiw, min(TOPK_MAX, (past + ls) // 4))
            heads = lambda a: a.reshape(db, ATTN_HEADS, ATTN_HEAD_DIM)
            o = _dsa_attn_sample(page_table, heads(qb), heads(k32), heads(v32), bias, cache_k[j], cache_v[j])
            xs = _out_ln(o.reshape(db, d), dsa_w_out_b, j, xs, g1, b1)
            ks_l.append(k32.reshape(db, ls, ATTN_HEADS, ATTN_HEAD_DIM))
            vs_l.append(v32.reshape(db, ls, ATTN_HEADS, ATTN_HEAD_DIM))
            kis_l.append(kiw[:, :IDX_DIM].reshape(db, ls, IDX_DIM))
        xp = _ffn_ln(xp, ffn_w_in_b, ffn_w_out_b, i, g2, b2)
        xs = _ffn_ln(xs, ffn_w_in_b, ffn_w_out_b, i, g2, b2)
    return (xp.reshape(bp, lp, d), xs.reshape(db, ls, d), ret_p, ret_s,
            jnp.stack(pool_p), jnp.stack(pool_s), jnp.stack(kp_l), jnp.stack(vp_l), jnp.stack(kip_l),
            jnp.stack(ks_l), jnp.stack(vs_l), jnp.stack(kis_l))
```
